```python
import math
import jax, jax.numpy as jnp
from jax import lax
import numpy as np


D_MODEL = 2048
BATCH = 1
SEQ = 8192
DEPTH = 4
DEC_BATCH = 8
DEC_SEQ = 2048
PAST_LEN = 128

GRID_W = 64
Q_BLOCK = 128
NORM_EPS = 1e-6
MASK_VALUE = -1e30

MLA_HEADS = 8
MLA_Q_RANK = 512
MLA_KV_RANK = 256
MLA_NOPE_DIM = 128
MLA_ROPE_DIM = 64
MLA_V_DIM = 128
MLA_ROPE_THETA = 10000.0

GQA_Q_HEADS = 8
GQA_KV_HEADS = 2
GQA_HEAD_DIM = 128
AXIAL_ROPE_THETA = 10000.0

DIL_HEADS = 16
DIL_HEAD_DIM = D_MODEL // DIL_HEADS
DIL_BRANCHES = ((128, 1), (512, 4), (2048, 16))

N_EXPERTS = 16
EXPERT_FF = 2048
CAPACITY_FACTOR = 2

N_EVEN = (DEPTH + 1) // 2
N_ODD = DEPTH // 2

AB_IN_DIMS = (MLA_Q_RANK, MLA_KV_RANK, MLA_ROPE_DIM,
              GQA_Q_HEADS * GQA_HEAD_DIM, GQA_KV_HEADS * GQA_HEAD_DIM, GQA_KV_HEADS * GQA_HEAD_DIM)
AB_IN_WIDTH = sum(AB_IN_DIMS)
AB_OUT_WIDTH = MLA_HEADS * MLA_V_DIM + GQA_Q_HEADS * GQA_HEAD_DIM
C_WIDTH = DIL_HEADS * DIL_HEAD_DIM

kernel_name = 'hybrid_mla_gqa_dilated_ec_encoder'


def rmsnorm(x, g):
    xf = x.astype(jnp.float32)
    y = xf * lax.rsqrt(jnp.mean(xf * xf, axis=-1, keepdims=True) + NORM_EPS)
    return (y * g.astype(jnp.float32)).astype(x.dtype)


def rotate(x, pos, theta):
    d = x.shape[-1]
    inv = jnp.power(theta, -jnp.arange(0, d, 2, dtype=jnp.float32) / d)
    ang = pos[:, None] * inv[None, :]
    cos = jnp.cos(ang)[None, :, None, :].astype(x.dtype)
    sin = jnp.sin(ang)[None, :, None, :].astype(x.dtype)
    x1, x2 = jnp.split(x, 2, axis=-1)
    return jnp.concatenate([x1 * cos - x2 * sin, x2 * cos + x1 * sin], axis=-1)


def axial_rotate(x, row_pos, col_pos):
    half = x.shape[-1] // 2
    return jnp.concatenate([rotate(x[..., :half], row_pos, AXIAL_ROPE_THETA),
                            rotate(x[..., half:], col_pos, AXIAL_ROPE_THETA)], axis=-1)


def block_attention(q, k, v):
    b, t, hq, dk = q.shape
    hk = k.shape[2]
    g = hq // hk
    scale = 1.0 / math.sqrt(dk)
    qb = q.reshape(b, t // Q_BLOCK, Q_BLOCK, hk, g, dk).transpose(1, 0, 2, 3, 4, 5)

    def one_block(qblk):
        s = jnp.einsum('bqhgd,bshd->bhgqs', qblk, k, preferred_element_type=jnp.float32) * scale
        p = jax.nn.softmax(s, axis=-1).astype(v.dtype)
        return jnp.einsum('bhgqs,bshd->bqhgd', p, v)

    o = lax.map(one_block, qb)
    return o.transpose(1, 0, 2, 3, 4, 5).reshape(b, t, hq, v.shape[-1])


def mla_gqa_mixer(h, w_in, mla_q_norm, mla_w_uq, mla_kv_norm, mla_w_ukv, mla_q_gain, mla_k_gain,
                  gqa_q_gain, gqa_k_gain, w_out):
    b, t, _ = h.shape
    proj = h @ w_in
    offsets = np.cumsum(AB_IN_DIMS)[:-1].tolist()
    cq, ckv, kpe, qb, kb, vb = jnp.split(proj, offsets, axis=-1)

    pos = jnp.arange(t, dtype=jnp.float32)
    q = (rmsnorm(cq, mla_q_norm) @ mla_w_uq).reshape(b, t, MLA_HEADS, MLA_NOPE_DIM + MLA_ROPE_DIM)
    kv = (rmsnorm(ckv, mla_kv_norm) @ mla_w_ukv).reshape(b, t, MLA_HEADS, MLA_NOPE_DIM + MLA_V_DIM)
    q_nope = rmsnorm(q[..., :MLA_NOPE_DIM], mla_q_gain[:MLA_NOPE_DIM])
    q_pe = rotate(rmsnorm(q[..., MLA_NOPE_DIM:], mla_q_gain[MLA_NOPE_DIM:]), pos, MLA_ROPE_THETA)
    k_nope = rmsnorm(kv[..., :MLA_NOPE_DIM], mla_k_gain[:MLA_NOPE_DIM])
    v_a = kv[..., MLA_NOPE_DIM:]
    k_pe = rotate(rmsnorm(kpe[:, :, None, :], mla_k_gain[MLA_NOPE_DIM:]), pos, MLA_ROPE_THETA)
    q_a = jnp.concatenate([q_nope, q_pe], axis=-1)
    k_a = jnp.concatenate([k_nope, jnp.broadcast_to(k_pe, (b, t, MLA_HEADS, MLA_ROPE_DIM))], axis=-1)
    o_a = block_attention(q_a, k_a, v_a)

    rows = t // GRID_W
    row_pos = jnp.repeat(jnp.arange(rows, dtype=jnp.float32), GRID_W)
    col_pos = jnp.tile(jnp.arange(GRID_W, dtype=jnp.float32), rows)
    q_b = axial_rotate(rmsnorm(qb.reshape(b, t, GQA_Q_HEADS, GQA_HEAD_DIM), gqa_q_gain), row_pos, col_pos)
    k_b = axial_rotate(rmsnorm(kb.reshape(b, t, GQA_KV_HEADS, GQA_HEAD_DIM), gqa_k_gain), row_pos, col_pos)
    v_b = vb.reshape(b, t, GQA_KV_HEADS, GQA_HEAD_DIM)
    o_b = block_attention(q_b, k_b, v_b)

    o = jnp.concatenate([o_a.reshape(b, t, -1), o_b.reshape(b, t, -1)], axis=-1)
    return o @ w_out


def alibi_slopes(n_heads):
    return jnp.power(2.0, -8.0 * jnp.arange(1, n_heads + 1, dtype=jnp.float32) / n_heads)


def dilated_branch(q, k, v, slopes, window, dilation):
    b, t, h, d = q.shape
    n = (window // 2) // dilation
    L = t // dilation

    def to_strided(x):
        return x.reshape(b, L, dilation, h, d).transpose(0, 2, 1, 3, 4).reshape(b * dilation, L, h, d)

    qs, ks, vs = to_strided(q), to_strided(k), to_strided(v)
    nb = -(-L // Q_BLOCK)
    lp = nb * Q_BLOCK
    qs = jnp.pad(qs, ((0, 0), (0, lp - L), (0, 0), (0, 0)))
    ks = jnp.pad(ks, ((0, 0), (n, n + lp - L), (0, 0), (0, 0)))
    vs = jnp.pad(vs, ((0, 0), (n, n + lp - L), (0, 0), (0, 0)))
    span = Q_BLOCK + 2 * n
    rel = jnp.arange(span)[None, :] - n - jnp.arange(Q_BLOCK)[:, None]
    in_window = jnp.abs(rel) <= n
    dist_bias = -slopes[:, None, None] * (dilation * jnp.abs(rel)).astype(jnp.float32)[None]
    scale = 1.0 / math.sqrt(d)

    def one_block(i):
        start = i * Q_BLOCK
        qb = lax.dynamic_slice_in_dim(qs, start, Q_BLOCK, axis=1)
        kb = lax.dynamic_slice_in_dim(ks, start, span, axis=1)
        vb = lax.dynamic_slice_in_dim(vs, start, span, axis=1)
        key_pos = start - n + jnp.arange(span)
        valid = in_window & ((key_pos >= 0) & (key_pos < L))[None, :]
        s = jnp.einsum('bqhd,bkhd->bhqk', qb, kb, preferred_element_type=jnp.float32) * scale + dist_bias
        s = jnp.where(valid, s, MASK_VALUE)
        m = jnp.max(s, axis=-1, keepdims=True)
        p = jnp.exp(s - m)
        den = jnp.sum(p, axis=-1, keepdims=True)
        o = jnp.einsum('bhqk,bkhd->bqhd', (p / den).astype(vb.dtype), vb)
        lse = (m + jnp.log(den))[..., 0].transpose(0, 2, 1)
        return o, lse

    o, lse = lax.map(one_block, jnp.arange(nb))
    o = o.transpose(1, 0, 2, 3, 4).reshape(b * dilation, lp, h, d)[:, :L]
    lse = lse.transpose(1, 0, 2, 3).reshape(b * dilation, lp, h)[:, :L]
    o = o.reshape(b, dilation, L, h, d).transpose(0, 2, 1, 3, 4).reshape(b, t, h, d)
    lse = lse.reshape(b, dilation, L, h).transpose(0, 2, 1, 3).reshape(b, t, h)
    return o, lse


def dilated_mixer(h, w_in, q_gain, k_gain, w_out):
    b, t, _ = h.shape
    qkv = (h @ w_in).reshape(b, t, 3, DIL_HEADS, DIL_HEAD_DIM)
    q = rmsnorm(qkv[:, :, 0], q_gain)
    k = rmsnorm(qkv[:, :, 1], k_gain)
    v = qkv[:, :, 2]
    slopes = alibi_slopes(DIL_HEADS)
    o1, l1 = dilated_branch(q, k, v, slopes, DIL_BRANCHES[0][0], DIL_BRANCHES[0][1])
    o2, l2 = dilated_branch(q, k, v, slopes, DIL_BRANCHES[1][0], DIL_BRANCHES[1][1])
    o3, l3 = dilated_branch(q, k, v, slopes, DIL_BRANCHES[2][0], DIL_BRANCHES[2][1])
    wts = jax.nn.softmax(jnp.stack([l1, l2, l3]), axis=0)
    outs = jnp.stack([o1, o2, o3]).astype(jnp.float32)
    o = jnp.sum(wts[..., None] * outs, axis=0).astype(h.dtype)
    return o.reshape(b, t, C_WIDTH) @ w_out


def expert_choice_ffn(h, w_router, w_gate, w_up, w_down):
    b, t, d = h.shape
    n_tok = b * t
    tok = h.reshape(n_tok, d)
    cap = CAPACITY_FACTOR * n_tok // N_EXPERTS
    aff = jax.nn.softmax(jnp.einsum('nd,de->ne', tok, w_router, preferred_element_type=jnp.float32), axis=-1)
    gate, idx = lax.top_k(aff.T, cap)
    xe = tok[idx]
    hid = jax.nn.silu(jnp.einsum('ecd,edf->ecf', xe, w_gate)) * jnp.einsum('ecd,edf->ecf', xe, w_up)
    ye = jnp.einsum('ecf,efd->ecd', hid, w_down) * gate[..., None].astype(h.dtype)
    out = jnp.zeros_like(tok).at[idx.reshape(-1)].add(ye.reshape(-1, d))
    return out.reshape(b, t, d)


def run_trunk(x, ln_mix, ln_ffn, w_in_ab, mla_q_norm, mla_w_uq, mla_kv_norm, mla_w_ukv, mla_q_gain,
              mla_k_gain, gqa_q_gain, gqa_k_gain, w_out_ab, w_in_c, c_q_gain, c_k_gain, w_out_c,
              w_router, w_gate, w_up, w_down):
    for layer in range(DEPTH):
        h = rmsnorm(x, ln_mix[layer])
        i = layer // 2
        if layer % 2 == 0:
            x = x + mla_gqa_mixer(h, w_in_ab[i], mla_q_norm[i], mla_w_uq[i], mla_kv_norm[i], mla_w_ukv[i],
                                  mla_q_gain[i], mla_k_gain[i], gqa_q_gain[i], gqa_k_gain[i], w_out_ab[i])
        else:
            x = x + dilated_mixer(h, w_in_c[i], c_q_gain[i], c_k_gain[i], w_out_c[i])
        x = x + expert_choice_ffn(rmsnorm(x, ln_ffn[layer]), w_router[layer], w_gate[layer],
                                  w_up[layer], w_down[layer])
    return x


def setup_inputs(seed: int = 0) -> dict:
    key = jax.random.key(seed)
    ks = jax.random.split(key, 24)
    f32 = jnp.float32

    def nrm(k, shape, fan_in):
        return jax.random.normal(k, shape, f32) * (fan_in ** -0.5)

    def gain(k, shape):
        return 1.0 + 0.02 * jax.random.normal(k, shape, f32)

    return {
        'x_prompt': jax.random.normal(ks[0], (BATCH, SEQ, D_MODEL), f32),
        'x_sample': jax.random.normal(ks[1], (DEC_BATCH, DEC_SEQ, D_MODEL), f32),
        'ln_mix': gain(ks[2], (DEPTH, D_MODEL)),
        'ln_ffn': gain(ks[3], (DEPTH, D_MODEL)),
        'w_in_ab': nrm(ks[4], (N_EVEN, D_MODEL, AB_IN_WIDTH), D_MODEL),
        'mla_q_norm': gain(ks[5], (N_EVEN, MLA_Q_RANK)),
        'mla_w_uq': nrm(ks[6], (N_EVEN, MLA_Q_RANK, MLA_HEADS * (MLA_NOPE_DIM + MLA_ROPE_DIM)), MLA_Q_RANK),
        'mla_kv_norm': gain(ks[7], (N_EVEN, MLA_KV_RANK)),
        'mla_w_ukv': nrm(ks[8], (N_EVEN, MLA_KV_RANK, MLA_HEADS * (MLA_NOPE_DIM + MLA_V_DIM)), MLA_KV_RANK),
        'mla_q_gain': gain(ks[9], (N_EVEN, MLA_NOPE_DIM + MLA_ROPE_DIM)),
        'mla_k_gain': gain(ks[10], (N_EVEN, MLA_NOPE_DIM + MLA_ROPE_DIM)),
        'gqa_q_gain': gain(ks[11], (N_EVEN, GQA_HEAD_DIM)),
        'gqa_k_gain': gain(ks[12], (N_EVEN, GQA_HEAD_DIM)),
        'w_out_ab': nrm(ks[13], (N_EVEN, AB_OUT_WIDTH, D_MODEL), AB_OUT_WIDTH),
        'w_in_c': nrm(ks[14], (N_ODD, D_MODEL, 3 * C_WIDTH), D_MODEL),
        'c_q_gain': gain(ks[15], (N_ODD, DIL_HEAD_DIM)),
        'c_k_gain': gain(ks[16], (N_ODD, DIL_HEAD_DIM)),
        'w_out_c': nrm(ks[17], (N_ODD, C_WIDTH, D_MODEL), C_WIDTH),
        'w_router': nrm(ks[18], (DEPTH, D_MODEL, N_EXPERTS), D_MODEL),
        'w_gate': nrm(ks[19], (DEPTH, N_EXPERTS, D_MODEL, EXPERT_FF), D_MODEL),
        'w_up': nrm(ks[20], (DEPTH, N_EXPERTS, D_MODEL, EXPERT_FF), D_MODEL),
        'w_down': nrm(ks[21], (DEPTH, N_EXPERTS, EXPERT_FF, D_MODEL), EXPERT_FF),
    }


def reference(x_prompt, x_sample, ln_mix, ln_ffn, w_in_ab, mla_q_norm, mla_w_uq, mla_kv_norm, mla_w_ukv,
              mla_q_gain, mla_k_gain, gqa_q_gain, gqa_k_gain, w_out_ab, w_in_c, c_q_gain, c_k_gain, w_out_c,
              w_router, w_gate, w_up, w_down):
    y_prompt = run_trunk(x_prompt, ln_mix, ln_ffn, w_in_ab, mla_q_norm, mla_w_uq, mla_kv_norm, mla_w_ukv,
                         mla_q_gain, mla_k_gain, gqa_q_gain, gqa_k_gain, w_out_ab, w_in_c, c_q_gain, c_k_gain,
                         w_out_c, w_router, w_gate, w_up, w_down)
    y_sample = run_trunk(x_sample, ln_mix, ln_ffn, w_in_ab, mla_q_norm, mla_w_uq, mla_kv_norm, mla_w_ukv,
                         mla_q_gain, mla_k_gain, gqa_q_gain, gqa_k_gain, w_out_ab, w_in_c, c_q_gain, c_k_gain,
                         w_out_c, w_router, w_gate, w_up, w_down)
    return (y_prompt, y_sample)
```

```python
import functools
import math

import jax
import jax.numpy as jnp
from jax import lax
from jax.experimental import pallas as pl
from jax.experimental.pallas import tpu as pltpu

F32 = jnp.float32
BF16 = jnp.bfloat16
I32 = jnp.int32
U32 = jnp.uint32

D_MODEL = 2048
NORM_EPS = 1e-6
MASK_VALUE = -1e30
GRID_W = 64

MLA_HEADS = 8
MLA_Q_RANK = 512
MLA_KV_RANK = 256
MLA_NOPE_DIM = 128
MLA_ROPE_DIM = 64
MLA_V_DIM = 128
GQA_Q_HEADS = 8
GQA_KV_HEADS = 2
GQA_HEAD_DIM = 128
ROPE_THETA = 10000.0

DIL_HEADS = 16
DIL_HEAD_DIM = 128
DIL_BRANCHES = ((128, 1), (512, 4), (2048, 16))
DIL_Q_BLOCK = 128

N_EXPERTS = 16
CAPACITY_FACTOR = 2

LANES = 128
SUBLANES = 8
VMEM_LIMIT_BYTES = 56 * 1024 * 1024

AB_COLS = dict(cq=0, ckv=512, kpe=768, qb=896, kb=1920, vb=2176)
AB_WIDTH = 2560


def _cparams(*sem):
    return pltpu.CompilerParams(dimension_semantics=sem, vmem_limit_bytes=VMEM_LIMIT_BYTES)


def _rmsnorm_body(x_ref, g_ref, o_ref):
    x = x_ref[...]
    ms = jnp.mean(x * x, axis=-1, keepdims=True)
    o_ref[...] = (x * lax.rsqrt(ms + NORM_EPS) * g_ref[...]).astype(o_ref.dtype)


def rmsnorm_rows(x, g, tm=512):
    n, d = x.shape
    return pl.pallas_call(
        _rmsnorm_body,
        grid=(n // tm,),
        in_specs=[pl.BlockSpec((tm, d), lambda i: (i, 0)), pl.BlockSpec((1, d), lambda i: (0, 0))],
        out_specs=pl.BlockSpec((tm, d), lambda i: (i, 0)),
        out_shape=jax.ShapeDtypeStruct((n, d), BF16),
        compiler_params=_cparams("parallel"),
        name="rmsnorm_rows",
    )(x, g.reshape(1, d))


def _matmul_body(*refs, has_res):
    if has_res:
        x_ref, w_ref, r_ref, o_ref = refs
    else:
        x_ref, w_ref, o_ref = refs
    acc = jnp.dot(x_ref[...], w_ref[...], preferred_element_type=F32)
    if has_res:
        acc = acc + r_ref[...]
    o_ref[...] = acc.astype(o_ref.dtype)


def matmul(x, w, res=None, tm=512, tn=None):
    n, k = x.shape
    m = w.shape[1]
    tn = m if tn is None else tn
    in_specs = [pl.BlockSpec((tm, k), lambda j, i: (i, 0)), pl.BlockSpec((k, tn), lambda j, i: (0, j))]
    args = [x, w]
    if res is not None:
        in_specs.append(pl.BlockSpec((tm, tn), lambda j, i: (i, j)))
        args.append(res)
    return pl.pallas_call(
        functools.partial(_matmul_body, has_res=res is not None),
        grid=(m // tn, n // tm),
        in_specs=in_specs,
        out_specs=pl.BlockSpec((tm, tn), lambda j, i: (i, j)),
        out_shape=jax.ShapeDtypeStruct((n, m), F32),
        compiler_params=_cparams("parallel", "parallel"),
        name="matmul",
    )(*args)


def _rms(x, g, n):
    return x * lax.rsqrt(jnp.sum(x * x, axis=-1, keepdims=True) * (1.0 / n) + NORM_EPS) * g


def _rotate_pairs(x, cos, sin):
    lane = lax.broadcasted_iota(I32, x.shape, 1)
    swapped = jnp.where((lane % 64) < 32, pltpu.roll(x, 96, 1), pltpu.roll(x, 32, 1))
    return x * cos + swapped * sin


def _prep_body(p_ref, wuq_ref, wukv_ref, qn_ref, kvn_ref, gq_ref, gk_ref, gbq_ref, gbk_ref,
               ca_ref, sa_ref, cb_ref, sb_ref,
               qa_ref, ka_ref, va_ref, qb_ref, kb_ref, vb_ref):
    c = AB_COLS
    scale_a = 1.0 / math.sqrt(MLA_NOPE_DIM + MLA_ROPE_DIM)
    scale_b = 1.0 / math.sqrt(GQA_HEAD_DIM)
    ca, sa, cb, sb = ca_ref[...], sa_ref[...], cb_ref[...], sb_ref[...]
    gq, gk = gq_ref[...], gk_ref[...]

    cqn = _rms(p_ref[:, c["cq"]:c["cq"] + MLA_Q_RANK], qn_ref[...], MLA_Q_RANK).astype(BF16)
    q = jnp.dot(cqn, wuq_ref[...], preferred_element_type=F32)
    ckvn = _rms(p_ref[:, c["ckv"]:c["ckv"] + MLA_KV_RANK], kvn_ref[...], MLA_KV_RANK).astype(BF16)
    kv = jnp.dot(ckvn, wukv_ref[...], preferred_element_type=F32)

    kpe = p_ref[:, c["kpe"]:c["kpe"] + LANES]
    kpe = _rotate_pairs(_rms(kpe, gk[:, LANES:], MLA_ROPE_DIM), ca, sa).astype(BF16)

    for h in range(MLA_HEADS):
        o = 2 * LANES * h
        qn = _rms(q[:, o:o + LANES], gq[:, :LANES], MLA_NOPE_DIM) * scale_a
        qr = _rotate_pairs(_rms(q[:, o + LANES:o + 2 * LANES], gq[:, LANES:], MLA_ROPE_DIM), ca, sa) * scale_a
        qa_ref[:, o:o + LANES] = qn.astype(BF16)
        qa_ref[:, o + LANES:o + 2 * LANES] = qr.astype(BF16)
        ka_ref[:, o:o + LANES] = _rms(kv[:, o:o + LANES], gk[:, :LANES], MLA_NOPE_DIM).astype(BF16)
        ka_ref[:, o + LANES:o + 2 * LANES] = kpe
        va_ref[:, LANES * h:LANES * (h + 1)] = kv[:, o + LANES:o + 2 * LANES].astype(BF16)

    for h in range(GQA_Q_HEADS):
        x = p_ref[:, c["qb"] + LANES * h:c["qb"] + LANES * (h + 1)]
        x = _rotate_pairs(_rms(x, gbq_ref[...], GQA_HEAD_DIM), cb, sb) * scale_b
        qb_ref[:, LANES * h:LANES * (h + 1)] = x.astype(BF16)
    for h in range(GQA_KV_HEADS):
        x = p_ref[:, c["kb"] + LANES * h:c["kb"] + LANES * (h + 1)]
        x = _rotate_pairs(_rms(x, gbk_ref[...], GQA_HEAD_DIM), cb, sb)
        kb_ref[:, LANES * h:LANES * (h + 1)] = x.astype(BF16)
        vb_ref[:, LANES * h:LANES * (h + 1)] = p_ref[:, c["vb"] + LANES * h:c["vb"] + LANES * (h + 1)].astype(BF16)


def mla_gqa_prep(proj, wuq, wukv, qn, kvn, gq, gk, gbq, gbk, tabs, tm=256):
    n = proj.shape[0]
    row = lambda w: pl.BlockSpec((tm, w), lambda i: (i, 0))
    full = lambda a: pl.BlockSpec(a.shape, lambda i: (0, 0))
    small = [wuq, wukv, qn, kvn, gq, gk, gbq, gbk]
    widths = [2 * LANES * MLA_HEADS, 2 * LANES * MLA_HEADS, LANES * MLA_HEADS,
              LANES * GQA_Q_HEADS, LANES * GQA_KV_HEADS, LANES * GQA_KV_HEADS]
    return pl.pallas_call(
        _prep_body,
        grid=(n // tm,),
        in_specs=[row(AB_WIDTH)] + [full(a) for a in small] + [row(LANES)] * 4,
        out_specs=[row(w) for w in widths],
        out_shape=[jax.ShapeDtypeStruct((n, w), BF16) for w in widths],
        compiler_params=_cparams("parallel"),
        name="mla_gqa_prep",
    )(proj, *small, *tabs)


def _attn_body(q_ref, k_ref, v_ref, o_ref, *, tk):
    q = q_ref[...]
    tq = q.shape[0]
    dv = v_ref.shape[1]
    n_kv = k_ref.shape[0] // tk

    def step(j, carry):
        m, l, acc = carry
        r0 = pl.multiple_of(j * tk, tk)
        k = k_ref[pl.ds(r0, tk), :]
        v = v_ref[pl.ds(r0, tk), :]
        s = lax.dot_general(q, k, (((1,), (1,)), ((), ())), preferred_element_type=F32)
        m_new = jnp.maximum(m, jnp.max(s, axis=-1, keepdims=True))
        alpha = jnp.exp(m - m_new)
        p = jnp.exp(s - m_new)
        l = alpha * l + jnp.sum(p, axis=-1, keepdims=True)
        acc = alpha * acc + jnp.dot(p.astype(BF16), v, preferred_element_type=F32)
        return m_new, l, acc

    init = (jnp.full((tq, 1), -jnp.inf, F32), jnp.zeros((tq, 1), F32), jnp.zeros((tq, dv), F32))
    m, l, acc = lax.fori_loop(0, n_kv, step, init)
    o_ref[...] = (acc / l).astype(o_ref.dtype)


def attention(q, k, v, segments, hq, hk, dk, dv, tq=512, tk=512):
    g = hq // hk
    outs = []
    for (off, n_seq, t) in segments:
        tq_s, tk_s = min(tq, t), min(tk, t)
        assert off % t == 0 and t % tq_s == 0 and t % tk_s == 0
        qblk = lambda b, h, i, off=off, t=t, tq_s=tq_s: ((off + b * t) // tq_s + i, h)
        oblk = lambda b, h, i, t=t, tq_s=tq_s: ((b * t) // tq_s + i, h)
        kvblk = lambda b, h, i, off=off, t=t: (off // t + b, h // g)
        outs.append(pl.pallas_call(
            functools.partial(_attn_body, tk=tk_s),
            grid=(n_seq, hq, t // tq_s),
            in_specs=[pl.BlockSpec((tq_s, dk), qblk), pl.BlockSpec((t, dk), kvblk), pl.BlockSpec((t, dv), kvblk)],
            out_specs=pl.BlockSpec((tq_s, dv), oblk),
            out_shape=jax.ShapeDtypeStruct((n_seq * t, hq * dv), BF16),
            compiler_params=_cparams("parallel", "parallel", "arbitrary"),
            name="attention",
        )(q, k, v))
    return outs


def _dilated_body(slope_ref, q_ref, k_ref, v_ref, gq_ref, gk_ref, o_ref, ob_ref, lb_ref, *, t, tile):
    h = pl.program_id(1)
    slope = slope_ref[h]
    gq, gk = gq_ref[...], gk_ref[...]
    scale = 1.0 / math.sqrt(DIL_HEAD_DIM)
    qb = DIL_Q_BLOCK

    def tile_body(ti, carry):
        base = pl.multiple_of(ti * tile, tile)
        for bi, (window, d) in enumerate(DIL_BRANCHES):
            nh = (window // 2) // d
            ln = t // d
            w = min(qb + 2 * nh, ln)
            for r in range(d):
                for bb in range(tile // (qb * d)):
                    p0 = ti * (tile // d) + qb * bb
                    ws = jnp.clip(p0 - nh, 0, ln - w)
                    q = q_ref[pl.ds(base + r + d * qb * bb, qb, stride=d), :]
                    kw = k_ref[pl.ds(r + d * ws, w, stride=d), :]
                    vw = v_ref[pl.ds(r + d * ws, w, stride=d), :]
                    qn = (_rms(q, gq, DIL_HEAD_DIM) * scale).astype(BF16)
                    kn = _rms(kw, gk, DIL_HEAD_DIM).astype(BF16)
                    s = lax.dot_general(qn, kn, (((1,), (1,)), ((), ())), preferred_element_type=F32)
                    qpos = p0 + lax.broadcasted_iota(I32, (qb, w), 0)
                    kpos = ws + lax.broadcasted_iota(I32, (qb, w), 1)
                    dist = jnp.abs(kpos - qpos)
                    s = s - slope * (d * dist).astype(F32)
                    s = jnp.where(dist <= nh, s, MASK_VALUE)
                    m = jnp.max(s, axis=-1, keepdims=True)
                    p = jnp.exp(s - m)
                    den = jnp.sum(p, axis=-1, keepdims=True)
                    o = jnp.dot((p / den).astype(BF16), vw.astype(BF16), preferred_element_type=F32)
                    lse = m + jnp.log(den)
                    rows = pl.ds(r + d * qb * bb, qb, stride=d)
                    ob_ref[bi, rows, :] = o
                    lb_ref[bi, rows, :] = jnp.broadcast_to(lse, (qb, DIL_HEAD_DIM))
        l0, l1, l2 = lb_ref[0], lb_ref[1], lb_ref[2]
        mx = jnp.maximum(jnp.maximum(l0, l1), l2)
        w0, w1, w2 = jnp.exp(l0 - mx), jnp.exp(l1 - mx), jnp.exp(l2 - mx)
        inv = 1.0 / (w0 + w1 + w2)
        merged = (w0 * inv) * ob_ref[0] + (w1 * inv) * ob_ref[1] + (w2 * inv) * ob_ref[2]
        o_ref[pl.ds(base, tile), :] = merged.astype(o_ref.dtype)
        return carry

    lax.fori_loop(0, t // tile, tile_body, 0)


def dilated_attention(qkv, gq, gk, slopes, segments):
    hd = DIL_HEAD_DIM
    max_d = max(d for _, d in DIL_BRANCHES)
    tile = DIL_Q_BLOCK * max_d
    outs = []
    for (off, n_seq, t) in segments:
        assert off % t == 0 and t % tile == 0
        blk = lambda c0: pl.BlockSpec((t, hd), lambda b, h, s, off=off, t=t, c0=c0: (off // t + b, c0 + h))
        vec = pl.BlockSpec((1, hd), lambda b, h, s: (0, 0))
        outs.append(pl.pallas_call(
            functools.partial(_dilated_body, t=t, tile=tile),
            grid_spec=pltpu.PrefetchScalarGridSpec(
                num_scalar_prefetch=1,
                grid=(n_seq, DIL_HEADS),
                in_specs=[blk(0), blk(DIL_HEADS), blk(2 * DIL_HEADS), vec, vec],
                out_specs=pl.BlockSpec((t, hd), lambda b, h, s: (b, h)),
                scratch_shapes=[pltpu.VMEM((len(DIL_BRANCHES), tile, hd), F32),
                                pltpu.VMEM((len(DIL_BRANCHES), tile, hd), F32)],
            ),
            out_shape=jax.ShapeDtypeStruct((n_seq * t, DIL_HEADS * hd), BF16),
            compiler_params=_cparams("parallel", "arbitrary"),
            name="dilated_attention",
        )(slopes, qkv, qkv, qkv, gq, gk))
    return outs


def _split_bf16(x):
    hi = x.astype(BF16)
    lo = (x - hi.astype(F32)).astype(BF16)
    return hi, lo


def _router_body(x_ref, g_ref, w_ref, hp_ref, aff_ref):
    x = x_ref[...]
    d = x.shape[1]
    ms = jnp.mean(x * x, axis=-1, keepdims=True)
    hn = x * lax.rsqrt(ms + NORM_EPS) * g_ref[...]
    xh, xl = _split_bf16(hn)
    wh, wl = _split_bf16(w_ref[...])
    logits = (jnp.dot(xh, wh, preferred_element_type=F32) + jnp.dot(xl, wh, preferred_element_type=F32)
              + jnp.dot(xh, wl, preferred_element_type=F32))
    mx = jnp.max(logits, axis=-1, keepdims=True)
    ex = jnp.exp(logits - mx)
    aff_ref[...] = ex / jnp.sum(ex, axis=-1, keepdims=True)
    r = pltpu.bitcast(xh.astype(F32), U32)
    hp_ref[...] = (r[:, :d // 2] >> 16) | (r[:, d // 2:] & jnp.uint32(0xFFFF0000))


def router(x, g, w_router, tm=512):
    n, d = x.shape
    e = w_router.shape[1]
    return pl.pallas_call(
        _router_body,
        grid=(n // tm,),
        in_specs=[pl.BlockSpec((tm, d), lambda i: (i, 0)), pl.BlockSpec((1, d), lambda i: (0, 0)),
                  pl.BlockSpec((d, e), lambda i: (0, 0))],
        out_specs=[pl.BlockSpec((tm, d // 2), lambda i: (i, 0)), pl.BlockSpec((tm, e), lambda i: (i, 0))],
        out_shape=[jax.ShapeDtypeStruct((n, d // 2), U32), jax.ShapeDtypeStruct((n, e), F32)],
        compiler_params=_cparams("parallel"),
        name="router",
    )(x, g.reshape(1, d), w_router)


ROUTE_CHUNK = 128
CUMSUM_CHUNK = 256


def _route_pos_body(aff_ref, posm_ref, pos_ref, sel_ref, *, cap):
    e, n = aff_ref.shape
    bits = pltpu.bitcast(aff_ref[...], I32)
    tok = lax.broadcasted_iota(I32, (e, n), 1)

    def count(mask):
        return jnp.sum(mask.astype(F32), axis=1, keepdims=True)

    def value_bit(i, thr):
        cand = thr | (jnp.int32(1) << (30 - i))
        return jnp.where(count(bits >= cand) >= cap, cand, thr)

    thr = lax.fori_loop(0, 31, value_bit, jnp.zeros((e, 1), I32))
    gt = bits > thr
    eq = bits == thr
    need = cap - count(gt)

    def index_bit(i, j):
        cand = j | (jnp.int32(1) << (n.bit_length() - 1 - i))
        return jnp.where(count(eq & (tok < cand)) < need, cand, j)

    last = lax.fori_loop(0, n.bit_length(), index_bit, jnp.zeros((e, 1), I32))
    sel = gt | (eq & (tok <= last))
    sel_ref[...] = sel.astype(F32)

    ck = CUMSUM_CHUNK
    triu = (lax.broadcasted_iota(I32, (ck, ck), 0) < lax.broadcasted_iota(I32, (ck, ck), 1)).astype(BF16)

    def cumsum_chunk(c, carry):
        c0 = pl.multiple_of(c * ck, ck)
        s = sel_ref[:, pl.ds(c0, ck)]
        excl = jnp.dot(s.astype(BF16), triu, preferred_element_type=F32) + carry
        pos_ref[:, pl.ds(c0, ck)] = excl.astype(I32)
        return carry + jnp.sum(s, axis=1, keepdims=True)

    lax.fori_loop(0, n // ck, cumsum_chunk, jnp.zeros((e, 1), F32))
    posm_ref[...] = jnp.where(sel_ref[...] > 0.5, pos_ref[...], -1)


def _route_compact_body(cs_ref, posm_ref, aff_ref, idx_ref, gate_ref, acci_ref, accg_ref):
    n, e = posm_ref.shape
    rc = ROUTE_CHUNK
    acci_ref[...] = jnp.zeros_like(acci_ref)
    accg_ref[...] = jnp.zeros_like(accg_ref)
    lane = lax.broadcasted_iota(I32, (rc, LANES), 1)
    row = lax.broadcasted_iota(I32, (rc, LANES), 0)

    def chunk(c, carry):
        r0 = pl.multiple_of(c * rc, rc)
        pm = posm_ref[pl.ds(r0, rc), :]
        af = aff_ref[pl.ds(r0, rc), :]
        tok = (r0 + row).astype(F32)
        for ex in range(e):
            pe = jnp.broadcast_to(pm[:, ex:ex + 1], (rc, LANES))
            ge = jnp.broadcast_to(af[:, ex:ex + 1], (rc, LANES))
            kb = cs_ref[c * e + ex] // LANES
            for half in range(2):
                s0 = pl.multiple_of((kb + half) * LANES, LANES)
                hit = pe == (s0 + lane)
                ci = jnp.where(hit, tok, 0.0).reshape(rc // SUBLANES, SUBLANES, LANES).sum(axis=0)
                cg = jnp.where(hit, ge, 0.0).reshape(rc // SUBLANES, SUBLANES, LANES).sum(axis=0)
                acci_ref[ex, :, pl.ds(s0, LANES)] += ci
                accg_ref[ex, :, pl.ds(s0, LANES)] += cg
        return carry

    lax.fori_loop(0, n // rc, chunk, 0)
    idx_ref[...] = jnp.sum(acci_ref[...], axis=1).astype(I32)
    gate_ref[...] = jnp.sum(accg_ref[...], axis=1)


def route_select(aff, aff_t, off, n_tok, cap):
    e = aff.shape[1]
    assert off % n_tok == 0 and n_tok % CUMSUM_CHUNK == 0 and cap % LANES == 0
    lane_dense = pl.BlockSpec((e, n_tok), lambda i: (0, 0))
    posm_t, pos_t = pl.pallas_call(
        functools.partial(_route_pos_body, cap=cap),
        grid=(1,),
        in_specs=[pl.BlockSpec((e, n_tok), lambda i: (0, off // n_tok))],
        out_specs=[lane_dense, lane_dense],
        out_shape=[jax.ShapeDtypeStruct((e, n_tok), I32), jax.ShapeDtypeStruct((e, n_tok), I32)],
        scratch_shapes=[pltpu.VMEM((e, n_tok), F32)],
        compiler_params=_cparams("arbitrary"),
        name="route_pos",
    )(aff_t)
    posm = posm_t.T
    cs = pos_t[:, ::ROUTE_CHUNK].T
    cpad = cap + 2 * LANES
    idx, gate = pl.pallas_call(
        _route_compact_body,
        grid_spec=pltpu.PrefetchScalarGridSpec(
            num_scalar_prefetch=1,
            grid=(1,),
            in_specs=[pl.BlockSpec((n_tok, e), lambda i, cs: (0, 0)),
                      pl.BlockSpec((n_tok, e), lambda i, cs: (off // n_tok, 0))],
            out_specs=[pl.BlockSpec((e, cpad), lambda i, cs: (0, 0)), pl.BlockSpec((e, cpad), lambda i, cs: (0, 0))],
            scratch_shapes=[pltpu.VMEM((e, SUBLANES, cpad), F32), pltpu.VMEM((e, SUBLANES, cpad), F32)],
        ),
        out_shape=[jax.ShapeDtypeStruct((e, cpad), I32), jax.ShapeDtypeStruct((e, cpad), F32)],
        compiler_params=_cparams("arbitrary"),
        name="route_compact",
    )(cs.reshape(-1), posm, aff)
    return idx[:, :cap], gate[:, :cap]


FFN_COL_CHUNK = 256
FFN_ROW_CHUNK = 512


def _unpack_bf16_pair(u):
    lo = pltpu.bitcast(u << 16, F32).astype(BF16)
    hi = pltpu.bitcast(u & jnp.uint32(0xFFFF0000), F32).astype(BF16)
    return lo, hi


def _ffn_body(idx_ref, hp_hbm, gate_ref, wg_ref, wu_ref, wd_ref, ye_ref, xg_ref, hid_ref, sem, *, rows, nf):
    ex = pl.program_id(0)
    s = pl.program_id(1)
    half = wg_ref.shape[0] // 2
    fc = wg_ref.shape[1]
    rc = FFN_ROW_CHUNK

    @pl.when(s == 0)
    def _gather():
        def issue(i, carry):
            t = idx_ref[ex * rows + i]
            pltpu.make_async_copy(hp_hbm.at[pl.ds(t, 1)], xg_ref.at[pl.ds(i, 1)], sem).start()
            return carry
        lax.fori_loop(0, rows, issue, 0)
        pltpu.make_async_copy(hp_hbm.at[pl.ds(0, rows)], xg_ref, sem).wait()

    @pl.when(s < nf)
    def _gate_up():
        wg_lo, wg_hi = wg_ref[:half, :].astype(BF16), wg_ref[half:, :].astype(BF16)
        wu_lo, wu_hi = wu_ref[:half, :].astype(BF16), wu_ref[half:, :].astype(BF16)
        col = pl.multiple_of(s * fc, fc)
        for r in range(rows // rc):
            x_lo, x_hi = _unpack_bf16_pair(xg_ref[r * rc:(r + 1) * rc, :])
            g = (jnp.dot(x_lo, wg_lo, preferred_element_type=F32)
                 + jnp.dot(x_hi, wg_hi, preferred_element_type=F32))
            u = (jnp.dot(x_lo, wu_lo, preferred_element_type=F32)
                 + jnp.dot(x_hi, wu_hi, preferred_element_type=F32))
            hid_ref[r * rc:(r + 1) * rc, pl.ds(col, fc)] = (g * jax.nn.sigmoid(g) * u).astype(BF16)

    @pl.when(s >= nf)
    def _down():
        wd = wd_ref[...].astype(BF16)
        for r in range(rows // rc):
            y = jnp.dot(hid_ref[r * rc:(r + 1) * rc, :], wd, preferred_element_type=F32)
            ye_ref[r * rc:(r + 1) * rc, :] = y * gate_ref[r * rc:(r + 1) * rc, :]


def expert_ffn(idx_flat, gate_col, hp, w_gate, w_up, w_down, layer, rows):
    n_e, d, f = w_gate.shape[1:]
    cc = FFN_COL_CHUNK
    nf, nd = f // cc, d // cc
    assert rows % FFN_ROW_CHUNK == 0
    wspec_up = pl.BlockSpec((None, None, d, cc), lambda ex, s, idx: (layer, ex, 0, jnp.minimum(s, nf - 1)))
    wspec_dn = pl.BlockSpec((None, None, f, cc), lambda ex, s, idx: (layer, ex, 0, jnp.maximum(s - nf, 0)))
    return pl.pallas_call(
        functools.partial(_ffn_body, rows=rows, nf=nf),
        grid_spec=pltpu.PrefetchScalarGridSpec(
            num_scalar_prefetch=1,
            grid=(n_e, nf + nd),
            in_specs=[pl.BlockSpec(memory_space=pl.ANY),
                      pl.BlockSpec((None, rows, 1), lambda ex, s, idx: (ex, 0, 0)),
                      wspec_up, wspec_up, wspec_dn],
            out_specs=pl.BlockSpec((None, rows, cc), lambda ex, s, idx: (ex, 0, jnp.maximum(s - nf, 0))),
            scratch_shapes=[pltpu.VMEM((rows, d // 2), U32), pltpu.VMEM((rows, f), BF16),
                            pltpu.SemaphoreType.DMA],
        ),
        out_shape=jax.ShapeDtypeStruct((n_e, rows, d), F32),
        compiler_params=_cparams("arbitrary", "arbitrary"),
        name="expert_ffn",
    )(idx_flat, hp, gate_col, w_gate, w_up, w_down)


COMBINE_ROWS = 256


def _combine_body(idx_ref, ye_ref, x_in_hbm, x_hbm, buf_ref, sem_in, sem_out, *, rows):
    del x_in_hbm
    ex = pl.program_id(0)
    j = pl.program_id(1)
    tm = COMBINE_ROWS
    base = ex * rows + j * tm

    def row_copy(i, to_hbm):
        t = idx_ref[base + i]
        if to_hbm:
            return pltpu.make_async_copy(buf_ref.at[pl.ds(i, 1)], x_hbm.at[pl.ds(t, 1)], sem_out)
        return pltpu.make_async_copy(x_hbm.at[pl.ds(t, 1)], buf_ref.at[pl.ds(i, 1)], sem_in)

    def issue_in(i, carry):
        row_copy(i, False).start()
        return carry
    lax.fori_loop(0, tm, issue_in, 0)
    pltpu.make_async_copy(x_hbm.at[pl.ds(0, tm)], buf_ref, sem_in).wait()
    buf_ref[...] += ye_ref[...]

    def issue_out(i, carry):
        row_copy(i, True).start()
        return carry
    lax.fori_loop(0, tm, issue_out, 0)
    pltpu.make_async_copy(buf_ref, x_hbm.at[pl.ds(0, tm)], sem_out).wait()


def combine(idx_flat, ye, x):
    n_e, rows, d = ye.shape
    tm = COMBINE_ROWS
    return pl.pallas_call(
        functools.partial(_combine_body, rows=rows),
        grid_spec=pltpu.PrefetchScalarGridSpec(
            num_scalar_prefetch=1,
            grid=(n_e, rows // tm),
            in_specs=[pl.BlockSpec((None, tm, d), lambda ex, j, idx: (ex, j, 0)),
                      pl.BlockSpec(memory_space=pl.ANY)],
            out_specs=pl.BlockSpec(memory_space=pl.ANY),
            scratch_shapes=[pltpu.VMEM((tm, d), F32), pltpu.SemaphoreType.DMA, pltpu.SemaphoreType.DMA],
        ),
        out_shape=jax.ShapeDtypeStruct(x.shape, x.dtype),
        input_output_aliases={2: 0},
        compiler_params=_cparams("arbitrary", "arbitrary"),
        name="combine",
    )(idx_flat, ye, x)


def _rope_tables(segments, n):
    half = MLA_ROPE_DIM // 2
    inv = jnp.power(ROPE_THETA, -jnp.arange(0, MLA_ROPE_DIM, 2, dtype=F32) / MLA_ROPE_DIM)
    pos = jnp.concatenate([jnp.tile(jnp.arange(t, dtype=F32), n_seq) for (_, n_seq, t) in segments])
    assert pos.shape[0] == n

    def cs(p):
        ang = p[:, None] * inv[None, :]
        return jnp.cos(ang), jnp.sin(ang)

    c, s = cs(pos)
    z = jnp.zeros((n, 2 * half), F32)
    cos_a = jnp.concatenate([c, c, z], axis=1)
    sin_a = jnp.concatenate([-s, s, z], axis=1)
    cr, sr = cs(jnp.floor(pos / GRID_W))
    cc, sc = cs(pos - GRID_W * jnp.floor(pos / GRID_W))
    cos_b = jnp.concatenate([cr, cr, cc, cc], axis=1)
    sin_b = jnp.concatenate([-sr, sr, -sc, sc], axis=1)
    return cos_a, sin_a, cos_b, sin_b


def _pad_cols(w, width):
    return jnp.pad(w, ((0, 0), (0, width - w.shape[1])))


def _even_layer_weights(w_in, w_uq, w_ukv, q_gain, k_gain):
    o = [0, MLA_Q_RANK, MLA_Q_RANK + MLA_KV_RANK]
    o.append(o[-1] + MLA_ROPE_DIM)
    o.append(o[-1] + GQA_Q_HEADS * GQA_HEAD_DIM)
    o.append(o[-1] + GQA_KV_HEADS * GQA_HEAD_DIM)
    o.append(o[-1] + GQA_KV_HEADS * GQA_HEAD_DIM)
    pieces = [w_in[:, o[0]:o[2]], _pad_cols(w_in[:, o[2]:o[3]], LANES), w_in[:, o[3]:o[6]]]
    w_in_p = _pad_cols(jnp.concatenate(pieces, axis=1), AB_WIDTH).astype(BF16)
    qd = MLA_NOPE_DIM + MLA_ROPE_DIM
    w_uq_p = jnp.pad(w_uq.reshape(MLA_Q_RANK, MLA_HEADS, qd), ((0, 0), (0, 0), (0, 2 * LANES - qd)))
    w_uq_p = w_uq_p.reshape(MLA_Q_RANK, MLA_HEADS * 2 * LANES).astype(BF16)
    gq = _pad_cols(q_gain.reshape(1, qd), 2 * LANES)
    gk = _pad_cols(k_gain.reshape(1, qd), 2 * LANES)
    return w_in_p, w_uq_p, w_ukv.astype(BF16), gq, gk


def _alibi_slopes():
    return jnp.power(2.0, -8.0 * jnp.arange(1, DIL_HEADS + 1, dtype=F32) / DIL_HEADS)


def kernel(x_prompt, x_sample, ln_mix, ln_ffn, w_in_ab, mla_q_norm, mla_w_uq, mla_kv_norm, mla_w_ukv, mla_q_gain, mla_k_gain, gqa_q_gain, gqa_k_gain, w_out_ab, w_in_c, c_q_gain, c_k_gain, w_out_c, w_router, w_gate, w_up, w_down):
    d = x_prompt.shape[-1]
    groups = [x_sample, x_prompt]
    segments, off = [], 0
    for g in groups:
        segments.append((off, g.shape[0], g.shape[1]))
        off += g.shape[0] * g.shape[1]
    n = off
    x = jnp.concatenate([g.reshape(-1, d) for g in groups], axis=0)
    depth = ln_mix.shape[0]
    n_e = w_router.shape[-1]
    caps = [CAPACITY_FACTOR * n_seq * t // n_e for (_, n_seq, t) in segments]
    rows = sum(caps)
    tabs = _rope_tables(segments, n)
    slopes = _alibi_slopes()

    for layer in range(depth):
        i = layer // 2
        h = rmsnorm_rows(x, ln_mix[layer])
        if layer % 2 == 0:
            w_in_p, w_uq_p, w_ukv_b, gq, gk = _even_layer_weights(
                w_in_ab[i], mla_w_uq[i], mla_w_ukv[i], mla_q_gain[i], mla_k_gain[i])
            proj = matmul(h, w_in_p, tn=AB_WIDTH // 2)
            qa, ka, va, qb, kb, vb = mla_gqa_prep(
                proj, w_uq_p, w_ukv_b, mla_q_norm[i].reshape(1, -1), mla_kv_norm[i].reshape(1, -1), gq, gk,
                gqa_q_gain[i].reshape(1, -1), gqa_k_gain[i].reshape(1, -1), tabs)
            o_a = attention(qa, ka, va, segments, MLA_HEADS, MLA_HEADS, 2 * LANES, MLA_V_DIM)
            o_b = attention(qb, kb, vb, segments, GQA_Q_HEADS, GQA_KV_HEADS, GQA_HEAD_DIM, GQA_HEAD_DIM)
            o = jnp.concatenate([jnp.concatenate([a, b], axis=1) for a, b in zip(o_a, o_b)], axis=0)
            x = matmul(o, w_out_ab[i].astype(BF16), res=x, tn=d // 2)
        else:
            qkv = matmul(h, w_in_c[i].astype(BF16), tn=w_in_c.shape[-1] // 4)
            o = jnp.concatenate(
                dilated_attention(qkv, c_q_gain[i].reshape(1, -1), c_k_gain[i].reshape(1, -1), slopes, segments),
                axis=0)
            x = matmul(o, w_out_c[i].astype(BF16), res=x, tn=d // 2)

        hp, aff = router(x, ln_ffn[layer], w_router[layer])
        aff_t = aff.T
        idx_parts, gate_parts = [], []
        for (off, n_seq, t), cap in zip(segments, caps):
            idx, gate = route_select(aff, aff_t, off, n_seq * t, cap)
            idx_parts.append(idx + off)
            gate_parts.append(gate)
        idx_flat = jnp.concatenate(idx_parts, axis=1).reshape(-1)
        gate_col = jnp.concatenate(gate_parts, axis=1).reshape(n_e, rows, 1)
        ye = expert_ffn(idx_flat, gate_col, hp, w_gate, w_up, w_down, layer, rows)
        x = combine(idx_flat, ye, x)

    outs = []
    for g, (off, n_seq, t) in zip(groups, segments):
        outs.append(x[off:off + n_seq * t].reshape(g.shape))
    return (outs[1], outs[0])
```

```python
import functools
import math

import jax
import jax.numpy as jnp
from jax import lax
from jax.experimental import pallas as pl
from jax.experimental.pallas import tpu as pltpu

F32 = jnp.float32
BF16 = jnp.bfloat16
I32 = jnp.int32
U32 = jnp.uint32

D_MODEL = 2048
NORM_EPS = 1e-6
MASK_VALUE = -1e30
GRID_W = 64

MLA_HEADS = 8
MLA_Q_RANK = 512
MLA_KV_RANK = 256
MLA_NOPE_DIM = 128
MLA_ROPE_DIM = 64
MLA_V_DIM = 128
GQA_Q_HEADS = 8
GQA_KV_HEADS = 2
GQA_HEAD_DIM = 128
ROPE_THETA = 10000.0

DIL_HEADS = 16
DIL_HEAD_DIM = 128
DIL_BRANCHES = ((128, 1), (512, 4), (2048, 16))
DIL_Q_BLOCK = 128
DIL_NORM_CHUNK = 256

N_EXPERTS = 16
CAPACITY_FACTOR = 2

LOG2_E = 1.4426950408889634
LANES = 128
SUBLANES = 8
VMEM_LIMIT_BYTES = 56 * 1024 * 1024

AB_COLS = dict(cq=0, ckv=512, kpe=768, qb=896, kb=1920, vb=2176)
AB_WIDTH = 2560


def _cparams(*sem):
    return pltpu.CompilerParams(dimension_semantics=sem, vmem_limit_bytes=VMEM_LIMIT_BYTES)


def _rmsnorm_body(x_ref, g_ref, o_ref):
    x = x_ref[...]
    ms = jnp.mean(x * x, axis=-1, keepdims=True)
    o_ref[...] = (x * lax.rsqrt(ms + NORM_EPS) * g_ref[...]).astype(o_ref.dtype)


def rmsnorm_rows(x, g, tm=512):
    n, d = x.shape
    return pl.pallas_call(
        _rmsnorm_body,
        grid=(n // tm,),
        in_specs=[pl.BlockSpec((tm, d), lambda i: (i, 0)), pl.BlockSpec((1, d), lambda i: (0, 0))],
        out_specs=pl.BlockSpec((tm, d), lambda i: (i, 0)),
        out_shape=jax.ShapeDtypeStruct((n, d), BF16),
        compiler_params=_cparams("parallel"),
        name="rmsnorm_rows",
    )(x, g.reshape(1, d))


def _matmul_body(*refs, has_res):
    if has_res:
        x_ref, w_ref, r_ref, o_ref = refs
    else:
        x_ref, w_ref, o_ref = refs
    acc = jnp.dot(x_ref[...], w_ref[...], preferred_element_type=F32)
    if has_res:
        acc = acc + r_ref[...]
    o_ref[...] = acc.astype(o_ref.dtype)


def matmul(x, w, res=None, tm=512, tn=None):
    n, k = x.shape
    m = w.shape[1]
    tn = m if tn is None else tn
    in_specs = [pl.BlockSpec((tm, k), lambda j, i: (i, 0)), pl.BlockSpec((k, tn), lambda j, i: (0, j))]
    args = [x, w]
    if res is not None:
        in_specs.append(pl.BlockSpec((tm, tn), lambda j, i: (i, j)))
        args.append(res)
    return pl.pallas_call(
        functools.partial(_matmul_body, has_res=res is not None),
        grid=(m // tn, n // tm),
        in_specs=in_specs,
        out_specs=pl.BlockSpec((tm, tn), lambda j, i: (i, j)),
        out_shape=jax.ShapeDtypeStruct((n, m), F32),
        compiler_params=_cparams("parallel", "parallel"),
        name="matmul",
    )(*args)


def _matmul_parts_body(*refs, n_seg, n_part, seg_tiles):
    x_refs = refs[:n_seg * n_part]
    w_ref, r_ref, o_ref = refs[n_seg * n_part:]
    i = pl.program_id(1)
    start = 0
    for s in range(n_seg):
        @pl.when((i >= start) & (i < start + seg_tiles[s]))
        def _segment(s=s):
            acc = r_ref[...]
            k0 = 0
            for p in range(n_part):
                x = x_refs[s * n_part + p][...]
                acc = acc + jnp.dot(x, w_ref[k0:k0 + x.shape[1], :], preferred_element_type=F32)
                k0 += x.shape[1]
            o_ref[...] = acc
        start += seg_tiles[s]


def matmul_parts(parts, w, res, tm=512, tn=None):
    n, m = res.shape
    k = w.shape[0]
    tn = m if tn is None else tn
    n_seg, n_part = len(parts), len(parts[0])
    seg_tiles = [seg[0].shape[0] // tm for seg in parts]
    assert sum(seg_tiles) * tm == n and all(seg[0].shape[0] % tm == 0 for seg in parts)
    in_specs, args, start = [], [], 0
    for s, seg in enumerate(parts):
        for x in seg:
            idx = lambda j, i, start=start, last=seg_tiles[s] - 1: (jnp.clip(i - start, 0, last), 0)
            in_specs.append(pl.BlockSpec((tm, x.shape[1]), idx))
            args.append(x)
        start += seg_tiles[s]
    in_specs += [pl.BlockSpec((k, tn), lambda j, i: (0, j)), pl.BlockSpec((tm, tn), lambda j, i: (i, j))]
    return pl.pallas_call(
        functools.partial(_matmul_parts_body, n_seg=n_seg, n_part=n_part, seg_tiles=tuple(seg_tiles)),
        grid=(m // tn, n // tm),
        in_specs=in_specs,
        out_specs=pl.BlockSpec((tm, tn), lambda j, i: (i, j)),
        out_shape=jax.ShapeDtypeStruct((n, m), F32),
        compiler_params=_cparams("parallel", "arbitrary"),
        name="matmul_parts",
    )(*args, w, res)


def _rms(x, g, n):
    return x * lax.rsqrt(jnp.sum(x * x, axis=-1, keepdims=True) * (1.0 / n) + NORM_EPS) * g


def _rotate_pairs(x, cos, sin):
    lane = lax.broadcasted_iota(I32, x.shape, 1)
    swapped = jnp.where((lane % 64) < 32, pltpu.roll(x, 96, 1), pltpu.roll(x, 32, 1))
    return x * cos + swapped * sin


def _prep_body(p_ref, wuq_ref, wukv_ref, qn_ref, kvn_ref, gq_ref, gk_ref, gbq_ref, gbk_ref,
               ca_ref, sa_ref, cb_ref, sb_ref,
               qa_ref, ka_ref, va_ref, qb_ref, kb_ref, vb_ref):
    c = AB_COLS
    scale_a = LOG2_E / math.sqrt(MLA_NOPE_DIM + MLA_ROPE_DIM)
    scale_b = LOG2_E / math.sqrt(GQA_HEAD_DIM)
    ca, sa, cb, sb = ca_ref[...], sa_ref[...], cb_ref[...], sb_ref[...]
    gq, gk = gq_ref[...], gk_ref[...]

    cqn = _rms(p_ref[:, c["cq"]:c["cq"] + MLA_Q_RANK], qn_ref[...], MLA_Q_RANK).astype(BF16)
    q = jnp.dot(cqn, wuq_ref[...], preferred_element_type=F32)
    ckvn = _rms(p_ref[:, c["ckv"]:c["ckv"] + MLA_KV_RANK], kvn_ref[...], MLA_KV_RANK).astype(BF16)
    kv = jnp.dot(ckvn, wukv_ref[...], preferred_element_type=F32)

    kpe = p_ref[:, c["kpe"]:c["kpe"] + LANES]
    kpe = _rotate_pairs(_rms(kpe, gk[:, LANES:], MLA_ROPE_DIM), ca, sa).astype(BF16)

    for h in range(MLA_HEADS):
        o = 2 * LANES * h
        qn = _rms(q[:, o:o + LANES], gq[:, :LANES], MLA_NOPE_DIM) * scale_a
        qr = _rotate_pairs(_rms(q[:, o + LANES:o + 2 * LANES], gq[:, LANES:], MLA_ROPE_DIM), ca, sa) * scale_a
        qa_ref[:, o:o + LANES] = qn.astype(BF16)
        qa_ref[:, o + LANES:o + 2 * LANES] = qr.astype(BF16)
        ka_ref[:, o:o + LANES] = _rms(kv[:, o:o + LANES], gk[:, :LANES], MLA_NOPE_DIM).astype(BF16)
        ka_ref[:, o + LANES:o + 2 * LANES] = kpe
        va_ref[:, LANES * h:LANES * (h + 1)] = kv[:, o + LANES:o + 2 * LANES].astype(BF16)

    for h in range(GQA_Q_HEADS):
        x = p_ref[:, c["qb"] + LANES * h:c["qb"] + LANES * (h + 1)]
        x = _rotate_pairs(_rms(x, gbq_ref[...], GQA_HEAD_DIM), cb, sb) * scale_b
        qb_ref[:, LANES * h:LANES * (h + 1)] = x.astype(BF16)
    for h in range(GQA_KV_HEADS):
        x = p_ref[:, c["kb"] + LANES * h:c["kb"] + LANES * (h + 1)]
        x = _rotate_pairs(_rms(x, gbk_ref[...], GQA_HEAD_DIM), cb, sb)
        kb_ref[:, LANES * h:LANES * (h + 1)] = x.astype(BF16)
        vb_ref[:, LANES * h:LANES * (h + 1)] = p_ref[:, c["vb"] + LANES * h:c["vb"] + LANES * (h + 1)].astype(BF16)


def mla_gqa_prep(proj, wuq, wukv, qn, kvn, gq, gk, gbq, gbk, tabs, tm=256):
    n = proj.shape[0]
    row = lambda w: pl.BlockSpec((tm, w), lambda i: (i, 0))
    full = lambda a: pl.BlockSpec(a.shape, lambda i: (0, 0))
    small = [wuq, wukv, qn, kvn, gq, gk, gbq, gbk]
    widths = [2 * LANES * MLA_HEADS, 2 * LANES * MLA_HEADS, LANES * MLA_HEADS,
              LANES * GQA_Q_HEADS, LANES * GQA_KV_HEADS, LANES * GQA_KV_HEADS]
    return pl.pallas_call(
        _prep_body,
        grid=(n // tm,),
        in_specs=[row(AB_WIDTH)] + [full(a) for a in small] + [row(LANES)] * 4,
        out_specs=[row(w) for w in widths],
        out_shape=[jax.ShapeDtypeStruct((n, w), BF16) for w in widths],
        compiler_params=_cparams("parallel"),
        name="mla_gqa_prep",
    )(proj, *small, *tabs)


def _attn_body(q_ref, k_ref, v_ref, o_ref, *, tk):
    q = q_ref[...]
    tq = q.shape[0]
    dv = v_ref.shape[1]
    n_kv = k_ref.shape[0] // tk

    def step(j, carry):
        m, l, acc = carry
        r0 = pl.multiple_of(j * tk, tk)
        k = k_ref[pl.ds(r0, tk), :]
        v = v_ref[pl.ds(r0, tk), :]
        s = lax.dot_general(q, k, (((1,), (1,)), ((), ())), preferred_element_type=F32)
        m_new = jnp.maximum(m, jnp.max(s, axis=-1, keepdims=True))
        alpha = jnp.exp2(m - m_new)
        p = jnp.exp2(s - m_new)
        l = alpha * l + jnp.sum(p, axis=-1, keepdims=True)
        acc = alpha * acc + jnp.dot(p.astype(BF16), v, preferred_element_type=F32)
        return m_new, l, acc

    init = (jnp.full((tq, 1), -jnp.inf, F32), jnp.zeros((tq, 1), F32), jnp.zeros((tq, dv), F32))
    m, l, acc = lax.fori_loop(0, n_kv, step, init)
    o_ref[...] = (acc / l).astype(o_ref.dtype)


def attention(q, k, v, segments, hq, hk, dk, dv, tq=1024, tk=1024):
    g = hq // hk
    outs = []
    for (off, n_seq, t) in segments:
        tq_s, tk_s = min(tq, t), min(tk, t)
        assert off % t == 0 and t % tq_s == 0 and t % tk_s == 0
        qblk = lambda b, h, i, off=off, t=t, tq_s=tq_s: ((off + b * t) // tq_s + i, h)
        oblk = lambda b, h, i, t=t, tq_s=tq_s: ((b * t) // tq_s + i, h)
        kvblk = lambda b, h, i, off=off, t=t: (off // t + b, h // g)
        outs.append(pl.pallas_call(
            functools.partial(_attn_body, tk=tk_s),
            grid=(n_seq, hq, t // tq_s),
            in_specs=[pl.BlockSpec((tq_s, dk), qblk), pl.BlockSpec((t, dk), kvblk), pl.BlockSpec((t, dv), kvblk)],
            out_specs=pl.BlockSpec((tq_s, dv), oblk),
            out_shape=jax.ShapeDtypeStruct((n_seq * t, hq * dv), BF16),
            compiler_params=_cparams("parallel", "parallel", "arbitrary"),
            name="attention",
        )(q, k, v))
    return outs


def _dilated_body(slope_ref, q_ref, k_ref, v_ref, gq_ref, gk_ref, o_ref, qn_ref, kn_ref, bias_ref, ob_ref, lb_ref,
                  *, t, tile):
    h = pl.program_id(1)
    slope = slope_ref[h]
    gq, gk = gq_ref[...], gk_ref[...]
    scale = 1.0 / math.sqrt(DIL_HEAD_DIM)
    qb = DIL_Q_BLOCK
    wmax = bias_ref.shape[-1]

    def norm_chunk(c, carry):
        rows = pl.ds(pl.multiple_of(c * DIL_NORM_CHUNK, DIL_NORM_CHUNK), DIL_NORM_CHUNK)
        qn_ref[rows, :] = _rms(q_ref[rows, :], gq, DIL_HEAD_DIM) * scale
        kn_ref[rows, :] = _rms(k_ref[rows, :], gk, DIL_HEAD_DIM)
        return carry

    lax.fori_loop(0, t // DIL_NORM_CHUNK, norm_chunk, 0)

    rel0 = lax.broadcasted_iota(I32, (qb, wmax), 1) - lax.broadcasted_iota(I32, (qb, wmax), 0)
    for bi, (window, d) in enumerate(DIL_BRANCHES):
        nh = (window // 2) // d
        for var in range(3):
            dist = jnp.abs(rel0 - nh * var)
            bias_ref[bi, var] = jnp.where(dist <= nh, -slope * (d * dist).astype(F32), MASK_VALUE)

    def tile_body(ti, carry):
        base = pl.multiple_of(ti * tile, tile)
        for bi, (window, d) in enumerate(DIL_BRANCHES):
            nh = (window // 2) // d
            ln = t // d
            w = min(qb + 2 * nh, ln)
            for r in range(d):
                for bb in range(tile // (qb * d)):
                    p0 = ti * (tile // d) + qb * bb
                    ws = jnp.clip(p0 - nh, 0, ln - w)
                    qn = qn_ref[pl.ds(base + r + d * qb * bb, qb, stride=d), :].astype(BF16)
                    kn = kn_ref[pl.ds(r + d * ws, w, stride=d), :].astype(BF16)
                    vw = v_ref[pl.ds(r + d * ws, w, stride=d), :]
                    s = lax.dot_general(qn, kn, (((1,), (1,)), ((), ())), preferred_element_type=F32)
                    s = s + bias_ref[bi, (p0 - ws) // nh, :, :w]
                    m = jnp.max(s, axis=-1, keepdims=True)
                    p = jnp.exp(s - m)
                    den = jnp.sum(p, axis=-1, keepdims=True)
                    o = jnp.dot((p / den).astype(BF16), vw.astype(BF16), preferred_element_type=F32)
                    lse = m + jnp.log(den)
                    rows = pl.ds(r + d * qb * bb, qb, stride=d)
                    ob_ref[bi, rows, :] = o
                    lb_ref[bi, rows, :] = jnp.broadcast_to(lse, (qb, DIL_HEAD_DIM))
        l0, l1, l2 = lb_ref[0], lb_ref[1], lb_ref[2]
        mx = jnp.maximum(jnp.maximum(l0, l1), l2)
        w0, w1, w2 = jnp.exp(l0 - mx), jnp.exp(l1 - mx), jnp.exp(l2 - mx)
        inv = 1.0 / (w0 + w1 + w2)
        merged = (w0 * inv) * ob_ref[0] + (w1 * inv) * ob_ref[1] + (w2 * inv) * ob_ref[2]
        o_ref[pl.ds(base, tile), :] = merged.astype(o_ref.dtype)
        return carry

    lax.fori_loop(0, t // tile, tile_body, 0)


def dilated_attention(qkv, gq, gk, slopes, segments):
    hd = DIL_HEAD_DIM
    max_d = max(d for _, d in DIL_BRANCHES)
    tile = DIL_Q_BLOCK * max_d
    max_nh = max((window // 2) // d for window, d in DIL_BRANCHES)
    assert max_nh <= DIL_Q_BLOCK
    wmax = DIL_Q_BLOCK + 2 * max_nh
    n_br = len(DIL_BRANCHES)
    outs = []
    for (off, n_seq, t) in segments:
        assert off % t == 0 and t % tile == 0 and t % DIL_NORM_CHUNK == 0
        blk = lambda c0: pl.BlockSpec((t, hd), lambda b, h, s, off=off, t=t, c0=c0: (off // t + b, c0 + h))
        vec = pl.BlockSpec((1, hd), lambda b, h, s: (0, 0))
        outs.append(pl.pallas_call(
            functools.partial(_dilated_body, t=t, tile=tile),
            grid_spec=pltpu.PrefetchScalarGridSpec(
                num_scalar_prefetch=1,
                grid=(n_seq, DIL_HEADS),
                in_specs=[blk(0), blk(DIL_HEADS), blk(2 * DIL_HEADS), vec, vec],
                out_specs=pl.BlockSpec((t, hd), lambda b, h, s: (b, h)),
                scratch_shapes=[pltpu.VMEM((t, hd), F32), pltpu.VMEM((t, hd), F32),
                                pltpu.VMEM((n_br, 3, DIL_Q_BLOCK, wmax), F32),
                                pltpu.VMEM((n_br, tile, hd), F32), pltpu.VMEM((n_br, tile, hd), F32)],
            ),
            out_shape=jax.ShapeDtypeStruct((n_seq * t, DIL_HEADS * hd), BF16),
            compiler_params=_cparams("parallel", "arbitrary"),
            name="dilated_attention",
        )(slopes, qkv, qkv, qkv, gq, gk))
    return outs


def _split_bf16(x):
    hi = x.astype(BF16)
    lo = (x - hi.astype(F32)).astype(BF16)
    return hi, lo


def _router_body(x_ref, g_ref, w_ref, hp_ref, aff_ref):
    x = x_ref[...]
    d = x.shape[1]
    ms = jnp.mean(x * x, axis=-1, keepdims=True)
    hn = x * lax.rsqrt(ms + NORM_EPS) * g_ref[...]
    xh, xl = _split_bf16(hn)
    wh, wl = _split_bf16(w_ref[...])
    logits = (jnp.dot(xh, wh, preferred_element_type=F32) + jnp.dot(xl, wh, preferred_element_type=F32)
              + jnp.dot(xh, wl, preferred_element_type=F32))
    mx = jnp.max(logits, axis=-1, keepdims=True)
    ex = jnp.exp(logits - mx)
    aff_ref[...] = ex / jnp.sum(ex, axis=-1, keepdims=True)
    r = pltpu.bitcast(xh.astype(F32), U32)
    hp_ref[...] = (r[:, :d // 2] >> 16) | (r[:, d // 2:] & jnp.uint32(0xFFFF0000))


def router(x, g, w_router, tm=512):
    n, d = x.shape
    e = w_router.shape[1]
    return pl.pallas_call(
        _router_body,
        grid=(n // tm,),
        in_specs=[pl.BlockSpec((tm, d), lambda i: (i, 0)), pl.BlockSpec((1, d), lambda i: (0, 0)),
                  pl.BlockSpec((d, e), lambda i: (0, 0))],
        out_specs=[pl.BlockSpec((tm, d // 2), lambda i: (i, 0)), pl.BlockSpec((tm, e), lambda i: (i, 0))],
        out_shape=[jax.ShapeDtypeStruct((n, d // 2), U32), jax.ShapeDtypeStruct((n, e), F32)],
        compiler_params=_cparams("parallel"),
        name="router",
    )(x, g.reshape(1, d), w_router)


ROUTE_CHUNK = 128
CUMSUM_CHUNK = 256


def _route_pos_body(aff_ref, posm_ref, pos_ref, sel_ref, *, cap):
    e, n = aff_ref.shape
    bits = pltpu.bitcast(aff_ref[...], I32)
    tok = lax.broadcasted_iota(I32, (e, n), 1)

    def count(mask):
        return jnp.sum(mask.astype(F32), axis=1, keepdims=True)

    def value_bit(i, thr):
        cand = thr | (jnp.int32(1) << (30 - i))
        return jnp.where(count(bits >= cand) >= cap, cand, thr)

    thr = lax.fori_loop(0, 31, value_bit, jnp.zeros((e, 1), I32))
    gt = bits > thr
    eq = bits == thr
    need = cap - count(gt)

    def index_bit(i, j):
        cand = j | (jnp.int32(1) << (n.bit_length() - 1 - i))
        return jnp.where(count(eq & (tok < cand)) < need, cand, j)

    last = lax.fori_loop(0, n.bit_length(), index_bit, jnp.zeros((e, 1), I32))
    sel = gt | (eq & (tok <= last))
    sel_ref[...] = sel.astype(F32)

    ck = CUMSUM_CHUNK
    triu = (lax.broadcasted_iota(I32, (ck, ck), 0) < lax.broadcasted_iota(I32, (ck, ck), 1)).astype(BF16)

    def cumsum_chunk(c, carry):
        c0 = pl.multiple_of(c * ck, ck)
        s = sel_ref[:, pl.ds(c0, ck)]
        excl = jnp.dot(s.astype(BF16), triu, preferred_element_type=F32) + carry
        pos_ref[:, pl.ds(c0, ck)] = excl.astype(I32)
        return carry + jnp.sum(s, axis=1, keepdims=True)

    lax.fori_loop(0, n // ck, cumsum_chunk, jnp.zeros((e, 1), F32))
    posm_ref[...] = jnp.where(sel_ref[...] > 0.5, pos_ref[...], -1)


def _route_compact_body(cs_ref, posm_ref, aff_ref, idx_ref, gate_ref, acci_ref, accg_ref):
    n, e = posm_ref.shape
    rc = ROUTE_CHUNK
    acci_ref[...] = jnp.zeros_like(acci_ref)
    accg_ref[...] = jnp.zeros_like(accg_ref)
    lane = lax.broadcasted_iota(I32, (rc, LANES), 1)
    row = lax.broadcasted_iota(I32, (rc, LANES), 0)

    def chunk(c, carry):
        r0 = pl.multiple_of(c * rc, rc)
        pm = posm_ref[pl.ds(r0, rc), :]
        af = aff_ref[pl.ds(r0, rc), :]
        tok = (r0 + row).astype(F32)
        for ex in range(e):
            pe = jnp.broadcast_to(pm[:, ex:ex + 1], (rc, LANES))
            ge = jnp.broadcast_to(af[:, ex:ex + 1], (rc, LANES))
            kb = cs_ref[c * e + ex] // LANES
            for half in range(2):
                s0 = pl.multiple_of((kb + half) * LANES, LANES)
                hit = pe == (s0 + lane)
                ci = jnp.where(hit, tok, 0.0).reshape(rc // SUBLANES, SUBLANES, LANES).sum(axis=0)
                cg = jnp.where(hit, ge, 0.0).reshape(rc // SUBLANES, SUBLANES, LANES).sum(axis=0)
                acci_ref[ex, :, pl.ds(s0, LANES)] += ci
                accg_ref[ex, :, pl.ds(s0, LANES)] += cg
        return carry

    lax.fori_loop(0, n // rc, chunk, 0)
    idx_ref[...] = jnp.sum(acci_ref[...], axis=1).astype(I32)
    gate_ref[...] = jnp.sum(accg_ref[...], axis=1)


def route_select(aff, aff_t, off, n_tok, cap):
    e = aff.shape[1]
    assert off % n_tok == 0 and n_tok % CUMSUM_CHUNK == 0 and cap % LANES == 0
    lane_dense = pl.BlockSpec((e, n_tok), lambda i: (0, 0))
    posm_t, pos_t = pl.pallas_call(
        functools.partial(_route_pos_body, cap=cap),
        grid=(1,),
        in_specs=[pl.BlockSpec((e, n_tok), lambda i: (0, off // n_tok))],
        out_specs=[lane_dense, lane_dense],
        out_shape=[jax.ShapeDtypeStruct((e, n_tok), I32), jax.ShapeDtypeStruct((e, n_tok), I32)],
        scratch_shapes=[pltpu.VMEM((e, n_tok), F32)],
        compiler_params=_cparams("arbitrary"),
        name="route_pos",
    )(aff_t)
    posm = posm_t.T
    cs = pos_t[:, ::ROUTE_CHUNK].T
    cpad = cap + 2 * LANES
    idx, gate = pl.pallas_call(
        _route_compact_body,
        grid_spec=pltpu.PrefetchScalarGridSpec(
            num_scalar_prefetch=1,
            grid=(1,),
            in_specs=[pl.BlockSpec((n_tok, e), lambda i, cs: (0, 0)),
                      pl.BlockSpec((n_tok, e), lambda i, cs: (off // n_tok, 0))],
            out_specs=[pl.BlockSpec((e, cpad), lambda i, cs: (0, 0)), pl.BlockSpec((e, cpad), lambda i, cs: (0, 0))],
            scratch_shapes=[pltpu.VMEM((e, SUBLANES, cpad), F32), pltpu.VMEM((e, SUBLANES, cpad), F32)],
        ),
        out_shape=[jax.ShapeDtypeStruct((e, cpad), I32), jax.ShapeDtypeStruct((e, cpad), F32)],
        compiler_params=_cparams("arbitrary"),
        name="route_compact",
    )(cs.reshape(-1), posm, aff)
    return idx[:, :cap], gate[:, :cap]


FFN_COL_CHUNK = 256
FFN_ROW_CHUNK = 512
FFN_GATHER_UNROLL = 32


def _unpack_bf16_pair(u):
    lo = pltpu.bitcast(u << 16, F32).astype(BF16)
    hi = pltpu.bitcast(u & jnp.uint32(0xFFFF0000), F32).astype(BF16)
    return lo, hi


def _ffn_body(idx_ref, hp_hbm, gate_ref, wg_ref, wu_ref, wd_ref, ye_ref, xg_ref, hid_ref, sem, *, rows, nf):
    ex = pl.program_id(0)
    s = pl.program_id(1)
    half = wg_ref.shape[0] // 2
    fc = wg_ref.shape[1]
    rc = FFN_ROW_CHUNK

    @pl.when(s == 0)
    def _gather():
        def issue(g, carry):
            g0 = pl.multiple_of(g * FFN_GATHER_UNROLL, FFN_GATHER_UNROLL)
            dst = xg_ref.at[pl.ds(g0, FFN_GATHER_UNROLL)]
            for u in range(FFN_GATHER_UNROLL):
                t = idx_ref[ex * rows + g0 + u]
                pltpu.make_async_copy(hp_hbm.at[pl.ds(t, 1)], dst.at[pl.ds(u, 1)], sem).start()
            return carry
        lax.fori_loop(0, rows // FFN_GATHER_UNROLL, issue, 0)
        pltpu.make_async_copy(hp_hbm.at[pl.ds(0, rows)], xg_ref, sem).wait()

    @pl.when(s < nf)
    def _gate_up():
        wg_lo, wg_hi = wg_ref[:half, :].astype(BF16), wg_ref[half:, :].astype(BF16)
        wu_lo, wu_hi = wu_ref[:half, :].astype(BF16), wu_ref[half:, :].astype(BF16)
        col = pl.multiple_of(s * fc, fc)
        for r in range(rows // rc):
            x_lo, x_hi = _unpack_bf16_pair(xg_ref[r * rc:(r + 1) * rc, :])
            g = (jnp.dot(x_lo, wg_lo, preferred_element_type=F32)
                 + jnp.dot(x_hi, wg_hi, preferred_element_type=F32))
            u = (jnp.dot(x_lo, wu_lo, preferred_element_type=F32)
                 + jnp.dot(x_hi, wu_hi, preferred_element_type=F32))
            hid_ref[r * rc:(r + 1) * rc, pl.ds(col, fc)] = (g * jax.nn.sigmoid(g) * u).astype(BF16)

    @pl.when(s >= nf)
    def _down():
        wd = wd_ref[...].astype(BF16)
        for r in range(rows // rc):
            y = jnp.dot(hid_ref[r * rc:(r + 1) * rc, :], wd, preferred_element_type=F32)
            ye_ref[r * rc:(r + 1) * rc, :] = y * gate_ref[r * rc:(r + 1) * rc, :]


def expert_ffn(idx_flat, gate_col, hp, w_gate, w_up, w_down, layer, rows):
    n_e, d, f = w_gate.shape[1:]
    cc = FFN_COL_CHUNK
    nf, nd = f // cc, d // cc
    assert rows % FFN_ROW_CHUNK == 0
    wspec_up = pl.BlockSpec((None, None, d, cc), lambda ex, s, idx: (layer, ex, 0, jnp.minimum(s, nf - 1)))
    wspec_dn = pl.BlockSpec((None, None, f, cc), lambda ex, s, idx: (layer, ex, 0, jnp.maximum(s - nf, 0)))
    return pl.pallas_call(
        functools.partial(_ffn_body, rows=rows, nf=nf),
        grid_spec=pltpu.PrefetchScalarGridSpec(
            num_scalar_prefetch=1,
            grid=(n_e, nf + nd),
            in_specs=[pl.BlockSpec(memory_space=pl.ANY),
                      pl.BlockSpec((None, rows, 1), lambda ex, s, idx: (ex, 0, 0)),
                      wspec_up, wspec_up, wspec_dn],
            out_specs=pl.BlockSpec((None, rows, cc), lambda ex, s, idx: (ex, 0, jnp.maximum(s - nf, 0))),
            scratch_shapes=[pltpu.VMEM((rows, d // 2), U32), pltpu.VMEM((rows, f), BF16),
                            pltpu.SemaphoreType.DMA],
        ),
        out_shape=jax.ShapeDtypeStruct((n_e, rows, d), F32),
        compiler_params=_cparams("arbitrary", "arbitrary"),
        name="expert_ffn",
    )(idx_flat, hp, gate_col, w_gate, w_up, w_down)


COMBINE_ROWS = 256
COMBINE_SLOTS = 3
COMBINE_MAX_TILES = 6


def _combine_body(idx_ref, ye_ref, x_in_hbm, x_hbm, buf_ref, sem_in, sem_out, *, rows, tiles):
    del x_in_hbm
    tm = COMBINE_ROWS
    base = pl.program_id(0) * rows + pl.program_id(1) * (tiles * tm)

    def gather(k):
        s = k % COMBINE_SLOTS
        for i in range(tm):
            t = idx_ref[base + k * tm + i]
            pltpu.make_async_copy(x_hbm.at[pl.ds(t, 1)], buf_ref.at[s, pl.ds(i, 1)], sem_in.at[s]).start()

    def scatter(k):
        s = k % COMBINE_SLOTS
        for i in range(tm):
            t = idx_ref[base + k * tm + i]
            pltpu.make_async_copy(buf_ref.at[s, pl.ds(i, 1)], x_hbm.at[pl.ds(t, 1)], sem_out.at[s]).start()

    def wait_gather(k):
        s = k % COMBINE_SLOTS
        pltpu.make_async_copy(x_hbm.at[pl.ds(0, tm)], buf_ref.at[s], sem_in.at[s]).wait()

    def wait_scatter(k):
        s = k % COMBINE_SLOTS
        pltpu.make_async_copy(buf_ref.at[s], x_hbm.at[pl.ds(0, tm)], sem_out.at[s]).wait()

    ahead = COMBINE_SLOTS - 1
    for k in range(min(ahead, tiles)):
        gather(k)
    drained = 0
    for k in range(tiles):
        s = k % COMBINE_SLOTS
        wait_gather(k)
        buf_ref[s] = buf_ref[s] + ye_ref[k * tm:(k + 1) * tm, :]
        scatter(k)
        if k + ahead < tiles:
            if k >= 1:
                wait_scatter(k - 1)
                drained = k
            gather(k + ahead)
    for k in range(drained, tiles):
        wait_scatter(k)


def combine(idx_flat, ye, x):
    n_e, rows, d = ye.shape
    tm = COMBINE_ROWS
    n_tiles = rows // tm
    assert rows % tm == 0
    tiles = max(c for c in range(1, COMBINE_MAX_TILES + 1) if n_tiles % c == 0)
    return pl.pallas_call(
        functools.partial(_combine_body, rows=rows, tiles=tiles),
        grid_spec=pltpu.PrefetchScalarGridSpec(
            num_scalar_prefetch=1,
            grid=(n_e, n_tiles // tiles),
            in_specs=[pl.BlockSpec((None, tiles * tm, d), lambda ex, j, idx: (ex, j, 0)),
                      pl.BlockSpec(memory_space=pl.ANY)],
            out_specs=pl.BlockSpec(memory_space=pl.ANY),
            scratch_shapes=[pltpu.VMEM((COMBINE_SLOTS, tm, d), F32), pltpu.SemaphoreType.DMA((COMBINE_SLOTS,)),
                            pltpu.SemaphoreType.DMA((COMBINE_SLOTS,))],
        ),
        out_shape=jax.ShapeDtypeStruct(x.shape, x.dtype),
        input_output_aliases={2: 0},
        compiler_params=_cparams("arbitrary", "arbitrary"),
        name="combine",
    )(idx_flat, ye, x)


def _rope_tables(segments, n):
    half = MLA_ROPE_DIM // 2
    inv = jnp.power(ROPE_THETA, -jnp.arange(0, MLA_ROPE_DIM, 2, dtype=F32) / MLA_ROPE_DIM)
    pos = jnp.concatenate([jnp.tile(jnp.arange(t, dtype=F32), n_seq) for (_, n_seq, t) in segments])
    assert pos.shape[0] == n

    def cs(p):
        ang = p[:, None] * inv[None, :]
        return jnp.cos(ang), jnp.sin(ang)

    c, s = cs(pos)
    z = jnp.zeros((n, 2 * half), F32)
    cos_a = jnp.concatenate([c, c, z], axis=1)
    sin_a = jnp.concatenate([-s, s, z], axis=1)
    cr, sr = cs(jnp.floor(pos / GRID_W))
    cc, sc = cs(pos - GRID_W * jnp.floor(pos / GRID_W))
    cos_b = jnp.concatenate([cr, cr, cc, cc], axis=1)
    sin_b = jnp.concatenate([-sr, sr, -sc, sc], axis=1)
    return cos_a, sin_a, cos_b, sin_b


def _pad_cols(w, width):
    return jnp.pad(w, ((0, 0), (0, width - w.shape[1])))


def _even_layer_weights(w_in, w_uq, w_ukv, q_gain, k_gain):
    o = [0, MLA_Q_RANK, MLA_Q_RANK + MLA_KV_RANK]
    o.append(o[-1] + MLA_ROPE_DIM)
    o.append(o[-1] + GQA_Q_HEADS * GQA_HEAD_DIM)
    o.append(o[-1] + GQA_KV_HEADS * GQA_HEAD_DIM)
    o.append(o[-1] + GQA_KV_HEADS * GQA_HEAD_DIM)
    pieces = [w_in[:, o[0]:o[2]], _pad_cols(w_in[:, o[2]:o[3]], LANES), w_in[:, o[3]:o[6]]]
    w_in_p = _pad_cols(jnp.concatenate(pieces, axis=1), AB_WIDTH).astype(BF16)
    qd = MLA_NOPE_DIM + MLA_ROPE_DIM
    w_uq_p = jnp.pad(w_uq.reshape(MLA_Q_RANK, MLA_HEADS, qd), ((0, 0), (0, 0), (0, 2 * LANES - qd)))
    w_uq_p = w_uq_p.reshape(MLA_Q_RANK, MLA_HEADS * 2 * LANES).astype(BF16)
    gq = _pad_cols(q_gain.reshape(1, qd), 2 * LANES)
    gk = _pad_cols(k_gain.reshape(1, qd), 2 * LANES)
    return w_in_p, w_uq_p, w_ukv.astype(BF16), gq, gk


def _alibi_slopes():
    return jnp.power(2.0, -8.0 * jnp.arange(1, DIL_HEADS + 1, dtype=F32) / DIL_HEADS)


def kernel(x_prompt, x_sample, ln_mix, ln_ffn, w_in_ab, mla_q_norm, mla_w_uq, mla_kv_norm, mla_w_ukv, mla_q_gain, mla_k_gain, gqa_q_gain, gqa_k_gain, w_out_ab, w_in_c, c_q_gain, c_k_gain, w_out_c, w_router, w_gate, w_up, w_down):
    d = x_prompt.shape[-1]
    groups = [x_sample, x_prompt]
    segments, off = [], 0
    for g in groups:
        segments.append((off, g.shape[0], g.shape[1]))
        off += g.shape[0] * g.shape[1]
    n = off
    x = jnp.concatenate([g.reshape(-1, d) for g in groups], axis=0)
    depth = ln_mix.shape[0]
    n_e = w_router.shape[-1]
    caps = [CAPACITY_FACTOR * n_seq * t // n_e for (_, n_seq, t) in segments]
    rows = sum(caps)
    tabs = _rope_tables(segments, n)
    slopes = _alibi_slopes()

    for layer in range(depth):
        i = layer // 2
        h = rmsnorm_rows(x, ln_mix[layer])
        if layer % 2 == 0:
            w_in_p, w_uq_p, w_ukv_b, gq, gk = _even_layer_weights(
                w_in_ab[i], mla_w_uq[i], mla_w_ukv[i], mla_q_gain[i], mla_k_gain[i])
            proj = matmul(h, w_in_p, tn=AB_WIDTH // 2)
            qa, ka, va, qb, kb, vb = mla_gqa_prep(
                proj, w_uq_p, w_ukv_b, mla_q_norm[i].reshape(1, -1), mla_kv_norm[i].reshape(1, -1), gq, gk,
                gqa_q_gain[i].reshape(1, -1), gqa_k_gain[i].reshape(1, -1), tabs)
            o_a = attention(qa, ka, va, segments, MLA_HEADS, MLA_HEADS, 2 * LANES, MLA_V_DIM)
            o_b = attention(qb, kb, vb, segments, GQA_Q_HEADS, GQA_KV_HEADS, GQA_HEAD_DIM, GQA_HEAD_DIM)
            x = matmul_parts([[a, b] for a, b in zip(o_a, o_b)], w_out_ab[i].astype(BF16), x, tn=d // 2)
        else:
            qkv = matmul(h, w_in_c[i].astype(BF16), tn=w_in_c.shape[-1] // 4)
            o = dilated_attention(qkv, c_q_gain[i].reshape(1, -1), c_k_gain[i].reshape(1, -1), slopes, segments)
            x = matmul_parts([[seg] for seg in o], w_out_c[i].astype(BF16), x, tn=d // 2)

        hp, aff = router(x, ln_ffn[layer], w_router[layer])
        aff_t = aff.T
        idx_parts, gate_parts = [], []
        for (off, n_seq, t), cap in zip(segments, caps):
            idx, gate = route_select(aff, aff_t, off, n_seq * t, cap)
            idx_parts.append(idx + off)
            gate_parts.append(gate)
        idx_flat = jnp.concatenate(idx_parts, axis=1).reshape(-1)
        gate_col = jnp.concatenate(gate_parts, axis=1).reshape(n_e, rows, 1)
        ye = expert_ffn(idx_flat, gate_col, hp, w_gate, w_up, w_down, layer, rows)
        x = combine(idx_flat, ye, x)

    outs = []
    for g, (off, n_seq, t) in zip(groups, segments):
        outs.append(x[off:off + n_seq * t].reshape(g.shape))
    return (outs[1], outs[0])
```

```python
import functools
import math

import jax
import jax.numpy as jnp
from jax import lax
from jax.experimental import pallas as pl
from jax.experimental.pallas import tpu as pltpu

F32 = jnp.float32
BF16 = jnp.bfloat16
I32 = jnp.int32
U32 = jnp.uint32

D_MODEL = 2048
NORM_EPS = 1e-6
MASK_VALUE = -1e30
GRID_W = 64

MLA_HEADS = 8
MLA_Q_RANK = 512
MLA_KV_RANK = 256
MLA_NOPE_DIM = 128
MLA_ROPE_DIM = 64
MLA_V_DIM = 128
GQA_Q_HEADS = 8
GQA_KV_HEADS = 2
GQA_HEAD_DIM = 128
ROPE_THETA = 10000.0

DIL_HEADS = 16
DIL_HEAD_DIM = 128
DIL_BRANCHES = ((128, 1), (512, 4), (2048, 16))
DIL_Q_BLOCK = 128
DIL_NORM_CHUNK = 256

N_EXPERTS = 16
CAPACITY_FACTOR = 2

LOG2_E = 1.4426950408889634
LANES = 128
SUBLANES = 8
VMEM_LIMIT_BYTES = 56 * 1024 * 1024

AB_COLS = dict(cq=0, ckv=512, kpe=768, qb=896, kb=1920, vb=2176)
AB_WIDTH = 2560


def _cparams(*sem):
    return pltpu.CompilerParams(dimension_semantics=sem, vmem_limit_bytes=VMEM_LIMIT_BYTES)


def _rmsnorm_body(x_ref, g_ref, o_ref):
    x = x_ref[...]
    ms = jnp.mean(x * x, axis=-1, keepdims=True)
    o_ref[...] = (x * lax.rsqrt(ms + NORM_EPS) * g_ref[...]).astype(o_ref.dtype)


def rmsnorm_rows(x, g, tm=512):
    n, d = x.shape
    return pl.pallas_call(
        _rmsnorm_body,
        grid=(n // tm,),
        in_specs=[pl.BlockSpec((tm, d), lambda i: (i, 0)), pl.BlockSpec((1, d), lambda i: (0, 0))],
        out_specs=pl.BlockSpec((tm, d), lambda i: (i, 0)),
        out_shape=jax.ShapeDtypeStruct((n, d), BF16),
        compiler_params=_cparams("parallel"),
        name="rmsnorm_rows",
    )(x, g.reshape(1, d))


def _matmul_body(x_ref, w_ref, o_ref):
    o_ref[...] = jnp.dot(x_ref[...], w_ref[...], preferred_element_type=F32)


def matmul(x, w, tm=512, tn=None):
    n, k = x.shape
    m = w.shape[1]
    tn = m if tn is None else tn
    return pl.pallas_call(
        _matmul_body,
        grid=(m // tn, n // tm),
        in_specs=[pl.BlockSpec((tm, k), lambda j, i: (i, 0)), pl.BlockSpec((k, tn), lambda j, i: (0, j))],
        out_specs=pl.BlockSpec((tm, tn), lambda j, i: (i, j)),
        out_shape=jax.ShapeDtypeStruct((n, m), F32),
        compiler_params=_cparams("parallel", "parallel"),
        name="matmul",
    )(x, w)


def _matmul_heads_body(x_ref, w_ref, o_ref):
    acc = jnp.dot(x_ref[...], w_ref[...], preferred_element_type=F32)
    for h in range(o_ref.shape[0]):
        o_ref[h] = acc[:, h * LANES:(h + 1) * LANES]


def matmul_heads(x, w, tm=512, tn=None):
    n, k = x.shape
    m = w.shape[1]
    tn = m if tn is None else tn
    return pl.pallas_call(
        _matmul_heads_body,
        grid=(m // tn, n // tm),
        in_specs=[pl.BlockSpec((tm, k), lambda j, i: (i, 0)), pl.BlockSpec((k, tn), lambda j, i: (0, j))],
        out_specs=pl.BlockSpec((tn // LANES, tm, LANES), lambda j, i: (j, i, 0)),
        out_shape=jax.ShapeDtypeStruct((m // LANES, n, LANES), F32),
        compiler_params=_cparams("parallel", "parallel"),
        name="matmul_heads",
    )(x, w)


def _matmul_parts_body(*refs, n_seg, n_part, seg_tiles):
    x_refs = refs[:n_seg * n_part]
    w_ref, r_ref, o_ref = refs[n_seg * n_part:]
    i = pl.program_id(1)
    start = 0
    for s in range(n_seg):
        @pl.when((i >= start) & (i < start + seg_tiles[s]))
        def _segment(s=s):
            acc = r_ref[...]
            k0 = 0
            for p in range(n_part):
                x = x_refs[s * n_part + p][...]
                acc = acc + jnp.dot(x, w_ref[k0:k0 + x.shape[1], :], preferred_element_type=F32)
                k0 += x.shape[1]
            o_ref[...] = acc
        start += seg_tiles[s]


def matmul_parts(parts, w, res, tm=512, tn=None):
    n, m = res.shape
    k = w.shape[0]
    tn = m if tn is None else tn
    n_seg, n_part = len(parts), len(parts[0])
    seg_tiles = [seg[0].shape[0] // tm for seg in parts]
    assert sum(seg_tiles) * tm == n and all(seg[0].shape[0] % tm == 0 for seg in parts)
    in_specs, args, start = [], [], 0
    for s, seg in enumerate(parts):
        for x in seg:
            idx = lambda j, i, start=start, last=seg_tiles[s] - 1: (jnp.clip(i - start, 0, last), 0)
            in_specs.append(pl.BlockSpec((tm, x.shape[1]), idx))
            args.append(x)
        start += seg_tiles[s]
    in_specs += [pl.BlockSpec((k, tn), lambda j, i: (0, j)), pl.BlockSpec((tm, tn), lambda j, i: (i, j))]
    return pl.pallas_call(
        functools.partial(_matmul_parts_body, n_seg=n_seg, n_part=n_part, seg_tiles=tuple(seg_tiles)),
        grid=(m // tn, n // tm),
        in_specs=in_specs,
        out_specs=pl.BlockSpec((tm, tn), lambda j, i: (i, j)),
        out_shape=jax.ShapeDtypeStruct((n, m), F32),
        compiler_params=_cparams("parallel", "arbitrary"),
        name="matmul_parts",
    )(*args, w, res)


def _rms(x, g, n):
    return x * lax.rsqrt(jnp.sum(x * x, axis=-1, keepdims=True) * (1.0 / n) + NORM_EPS) * g


def _rotate_pairs(x, cos, sin):
    lane = lax.broadcasted_iota(I32, x.shape, 1)
    swapped = jnp.where((lane % 64) < 32, pltpu.roll(x, 96, 1), pltpu.roll(x, 32, 1))
    return x * cos + swapped * sin


def _prep_body(p_ref, wuq_ref, wukv_ref, qn_ref, kvn_ref, gq_ref, gk_ref, gbq_ref, gbk_ref,
               ca_ref, sa_ref, cb_ref, sb_ref,
               qa_ref, ka_ref, va_ref, qb_ref, kb_ref, vb_ref):
    c = AB_COLS
    scale_a = LOG2_E / math.sqrt(MLA_NOPE_DIM + MLA_ROPE_DIM)
    scale_b = LOG2_E / math.sqrt(GQA_HEAD_DIM)
    ca, sa, cb, sb = ca_ref[...], sa_ref[...], cb_ref[...], sb_ref[...]
    gq, gk = gq_ref[...], gk_ref[...]

    cqn = _rms(p_ref[:, c["cq"]:c["cq"] + MLA_Q_RANK], qn_ref[...], MLA_Q_RANK).astype(BF16)
    q = jnp.dot(cqn, wuq_ref[...], preferred_element_type=F32)
    ckvn = _rms(p_ref[:, c["ckv"]:c["ckv"] + MLA_KV_RANK], kvn_ref[...], MLA_KV_RANK).astype(BF16)
    kv = jnp.dot(ckvn, wukv_ref[...], preferred_element_type=F32)

    kpe = p_ref[:, c["kpe"]:c["kpe"] + LANES]
    kpe = _rotate_pairs(_rms(kpe, gk[:, LANES:], MLA_ROPE_DIM), ca, sa).astype(BF16)

    for h in range(MLA_HEADS):
        o = 2 * LANES * h
        qn = _rms(q[:, o:o + LANES], gq[:, :LANES], MLA_NOPE_DIM) * scale_a
        qr = _rotate_pairs(_rms(q[:, o + LANES:o + 2 * LANES], gq[:, LANES:], MLA_ROPE_DIM), ca, sa) * scale_a
        qa_ref[:, o:o + LANES] = qn.astype(BF16)
        qa_ref[:, o + LANES:o + 2 * LANES] = qr.astype(BF16)
        ka_ref[:, o:o + LANES] = _rms(kv[:, o:o + LANES], gk[:, :LANES], MLA_NOPE_DIM).astype(BF16)
        ka_ref[:, o + LANES:o + 2 * LANES] = kpe
        va_ref[:, LANES * h:LANES * (h + 1)] = kv[:, o + LANES:o + 2 * LANES].astype(BF16)

    for h in range(GQA_Q_HEADS):
        x = p_ref[:, c["qb"] + LANES * h:c["qb"] + LANES * (h + 1)]
        x = _rotate_pairs(_rms(x, gbq_ref[...], GQA_HEAD_DIM), cb, sb) * scale_b
        qb_ref[:, LANES * h:LANES * (h + 1)] = x.astype(BF16)
    for h in range(GQA_KV_HEADS):
        x = p_ref[:, c["kb"] + LANES * h:c["kb"] + LANES * (h + 1)]
        x = _rotate_pairs(_rms(x, gbk_ref[...], GQA_HEAD_DIM), cb, sb)
        kb_ref[:, LANES * h:LANES * (h + 1)] = x.astype(BF16)
        vb_ref[:, LANES * h:LANES * (h + 1)] = p_ref[:, c["vb"] + LANES * h:c["vb"] + LANES * (h + 1)].astype(BF16)


def mla_gqa_prep(proj, wuq, wukv, qn, kvn, gq, gk, gbq, gbk, tabs, tm=256):
    n = proj.shape[0]
    row = lambda w: pl.BlockSpec((tm, w), lambda i: (i, 0))
    full = lambda a: pl.BlockSpec(a.shape, lambda i: (0, 0))
    small = [wuq, wukv, qn, kvn, gq, gk, gbq, gbk]
    widths = [2 * LANES * MLA_HEADS, 2 * LANES * MLA_HEADS, LANES * MLA_HEADS,
              LANES * GQA_Q_HEADS, LANES * GQA_KV_HEADS, LANES * GQA_KV_HEADS]
    return pl.pallas_call(
        _prep_body,
        grid=(n // tm,),
        in_specs=[row(AB_WIDTH)] + [full(a) for a in small] + [row(LANES)] * 4,
        out_specs=[row(w) for w in widths],
        out_shape=[jax.ShapeDtypeStruct((n, w), BF16) for w in widths],
        compiler_params=_cparams("parallel"),
        name="mla_gqa_prep",
    )(proj, *small, *tabs)


def _attn_body(q_ref, k_ref, v_ref, o_ref, *, tk):
    q = q_ref[...]
    tq = q.shape[0]
    dv = v_ref.shape[1]
    n_kv = k_ref.shape[0] // tk

    def step(j, carry):
        m, l, acc = carry
        r0 = pl.multiple_of(j * tk, tk)
        k = k_ref[pl.ds(r0, tk), :]
        v = v_ref[pl.ds(r0, tk), :]
        s = lax.dot_general(q, k, (((1,), (1,)), ((), ())), preferred_element_type=F32)
        m_new = jnp.maximum(m, jnp.max(s, axis=-1, keepdims=True))
        alpha = jnp.exp2(m - m_new)
        p = jnp.exp2(s - m_new)
        l = alpha * l + jnp.sum(p, axis=-1, keepdims=True)
        acc = alpha * acc + jnp.dot(p.astype(BF16), v, preferred_element_type=F32)
        return m_new, l, acc

    init = (jnp.full((tq, 1), -jnp.inf, F32), jnp.zeros((tq, 1), F32), jnp.zeros((tq, dv), F32))
    m, l, acc = lax.fori_loop(0, n_kv, step, init)
    o_ref[...] = (acc / l).astype(o_ref.dtype)


def attention(q, k, v, segments, hq, hk, dk, dv, tq=1024, tk=1024):
    g = hq // hk
    outs = []
    for (off, n_seq, t) in segments:
        tq_s, tk_s = min(tq, t), min(tk, t)
        assert off % t == 0 and t % tq_s == 0 and t % tk_s == 0
        qblk = lambda b, h, i, off=off, t=t, tq_s=tq_s: ((off + b * t) // tq_s + i, h)
        oblk = lambda b, h, i, t=t, tq_s=tq_s: ((b * t) // tq_s + i, h)
        kvblk = lambda b, h, i, off=off, t=t: (off // t + b, h // g)
        outs.append(pl.pallas_call(
            functools.partial(_attn_body, tk=tk_s),
            grid=(n_seq, hq, t // tq_s),
            in_specs=[pl.BlockSpec((tq_s, dk), qblk), pl.BlockSpec((t, dk), kvblk), pl.BlockSpec((t, dv), kvblk)],
            out_specs=pl.BlockSpec((tq_s, dv), oblk),
            out_shape=jax.ShapeDtypeStruct((n_seq * t, hq * dv), BF16),
            compiler_params=_cparams("parallel", "parallel", "arbitrary"),
            name="attention",
        )(q, k, v))
    return outs


def _dilated_body(slope_ref, q_ref, k_ref, v_ref, gq_ref, gk_ref, o_ref, qn_ref, kn_ref, bias_ref, ob_ref, lb_ref,
                  *, t, tile):
    h = pl.program_id(1)
    slope = slope_ref[h]
    gq, gk = gq_ref[...], gk_ref[...]
    scale = 1.0 / math.sqrt(DIL_HEAD_DIM)
    qb = DIL_Q_BLOCK
    wmax = bias_ref.shape[-1]

    def norm_chunk(c, carry):
        rows = pl.ds(pl.multiple_of(c * DIL_NORM_CHUNK, DIL_NORM_CHUNK), DIL_NORM_CHUNK)
        qn_ref[rows, :] = _rms(q_ref[rows, :], gq, DIL_HEAD_DIM) * scale
        kn_ref[rows, :] = _rms(k_ref[rows, :], gk, DIL_HEAD_DIM)
        return carry

    lax.fori_loop(0, t // DIL_NORM_CHUNK, norm_chunk, 0)

    rel0 = lax.broadcasted_iota(I32, (qb, wmax), 1) - lax.broadcasted_iota(I32, (qb, wmax), 0)
    for bi, (window, d) in enumerate(DIL_BRANCHES):
        nh = (window // 2) // d
        for var in range(3):
            dist = jnp.abs(rel0 - nh * var)
            bias_ref[bi, var] = jnp.where(dist <= nh, -slope * (d * dist).astype(F32), MASK_VALUE)

    def tile_body(ti, carry):
        base = pl.multiple_of(ti * tile, tile)
        for bi, (window, d) in enumerate(DIL_BRANCHES):
            nh = (window // 2) // d
            ln = t // d
            w = min(qb + 2 * nh, ln)
            for r in range(d):
                for bb in range(tile // (qb * d)):
                    p0 = ti * (tile // d) + qb * bb
                    ws = jnp.clip(p0 - nh, 0, ln - w)
                    qn = qn_ref[pl.ds(base + r + d * qb * bb, qb, stride=d), :].astype(BF16)
                    kn = kn_ref[pl.ds(r + d * ws, w, stride=d), :].astype(BF16)
                    vw = v_ref[pl.ds(r + d * ws, w, stride=d), :]
                    s = lax.dot_general(qn, kn, (((1,), (1,)), ((), ())), preferred_element_type=F32)
                    s = s + bias_ref[bi, (p0 - ws) // nh, :, :w]
                    m = jnp.max(s, axis=-1, keepdims=True)
                    p = jnp.exp(s - m).astype(BF16)
                    v1 = jnp.concatenate([vw.astype(BF16), jnp.ones((w, DIL_HEAD_DIM), BF16)], axis=1)
                    od = jnp.dot(p, v1, preferred_element_type=F32)
                    den = od[:, DIL_HEAD_DIM:]
                    rows = pl.ds(r + d * qb * bb, qb, stride=d)
                    ob_ref[bi, rows, :] = od[:, :DIL_HEAD_DIM] / den
                    lb_ref[bi, rows, :] = m + jnp.log(den)
        l0, l1, l2 = lb_ref[0], lb_ref[1], lb_ref[2]
        mx = jnp.maximum(jnp.maximum(l0, l1), l2)
        w0, w1, w2 = jnp.exp(l0 - mx), jnp.exp(l1 - mx), jnp.exp(l2 - mx)
        inv = 1.0 / (w0 + w1 + w2)
        merged = (w0 * inv) * ob_ref[0] + (w1 * inv) * ob_ref[1] + (w2 * inv) * ob_ref[2]
        o_ref[pl.ds(base, tile), :] = merged.astype(o_ref.dtype)
        return carry

    lax.fori_loop(0, t // tile, tile_body, 0)


def dilated_attention(qkv, gq, gk, slopes, segments):
    hd = DIL_HEAD_DIM
    max_d = max(d for _, d in DIL_BRANCHES)
    tile = DIL_Q_BLOCK * max_d
    max_nh = max((window // 2) // d for window, d in DIL_BRANCHES)
    assert max_nh <= DIL_Q_BLOCK
    wmax = DIL_Q_BLOCK + 2 * max_nh
    n_br = len(DIL_BRANCHES)
    outs = []
    for (off, n_seq, t) in segments:
        assert off % t == 0 and t % tile == 0 and t % DIL_NORM_CHUNK == 0
        blk = lambda c0: pl.BlockSpec((None, t, hd), lambda b, h, s, off=off, t=t, c0=c0: (c0 + h, off // t + b, 0))
        vec = pl.BlockSpec((1, hd), lambda b, h, s: (0, 0))
        outs.append(pl.pallas_call(
            functools.partial(_dilated_body, t=t, tile=tile),
            grid_spec=pltpu.PrefetchScalarGridSpec(
                num_scalar_prefetch=1,
                grid=(n_seq, DIL_HEADS),
                in_specs=[blk(0), blk(DIL_HEADS), blk(2 * DIL_HEADS), vec, vec],
                out_specs=pl.BlockSpec((t, hd), lambda b, h, s: (b, h)),
                scratch_shapes=[pltpu.VMEM((t, hd), F32), pltpu.VMEM((t, hd), F32),
                                pltpu.VMEM((n_br, 3, DIL_Q_BLOCK, wmax), F32),
                                pltpu.VMEM((n_br, tile, hd), F32), pltpu.VMEM((n_br, tile, hd), F32)],
            ),
            out_shape=jax.ShapeDtypeStruct((n_seq * t, DIL_HEADS * hd), BF16),
            compiler_params=_cparams("parallel", "arbitrary"),
            name="dilated_attention",
        )(slopes, qkv, qkv, qkv, gq, gk))
    return outs


def _split_bf16(x):
    hi = x.astype(BF16)
    lo = (x - hi.astype(F32)).astype(BF16)
    return hi, lo


def _router_body(x_ref, g_ref, w_ref, hp_ref, aff_ref):
    x = x_ref[...]
    d = x.shape[1]
    ms = jnp.mean(x * x, axis=-1, keepdims=True)
    hn = x * lax.rsqrt(ms + NORM_EPS) * g_ref[...]
    xh, xl = _split_bf16(hn)
    wh, wl = _split_bf16(w_ref[...])
    logits = (jnp.dot(xh, wh, preferred_element_type=F32) + jnp.dot(xl, wh, preferred_element_type=F32)
              + jnp.dot(xh, wl, preferred_element_type=F32))
    mx = jnp.max(logits, axis=-1, keepdims=True)
    ex = jnp.exp(logits - mx)
    aff_ref[...] = ex / jnp.sum(ex, axis=-1, keepdims=True)
    r = pltpu.bitcast(xh.astype(F32), U32)
    hp_ref[...] = (r[:, :d // 2] >> 16) | (r[:, d // 2:] & jnp.uint32(0xFFFF0000))


def router(x, g, w_router, tm=512):
    n, d = x.shape
    e = w_router.shape[1]
    return pl.pallas_call(
        _router_body,
        grid=(n // tm,),
        in_specs=[pl.BlockSpec((tm, d), lambda i: (i, 0)), pl.BlockSpec((1, d), lambda i: (0, 0)),
                  pl.BlockSpec((d, e), lambda i: (0, 0))],
        out_specs=[pl.BlockSpec((tm, d // 2), lambda i: (i, 0)), pl.BlockSpec((tm, e), lambda i: (i, 0))],
        out_shape=[jax.ShapeDtypeStruct((n, d // 2), U32), jax.ShapeDtypeStruct((n, e), F32)],
        compiler_params=_cparams("parallel"),
        name="router",
    )(x, g.reshape(1, d), w_router)


ROUTE_CHUNK = 128
CUMSUM_CHUNK = 256


def _route_pos_body(aff_ref, posm_ref, pos_ref, sel_ref, *, cap):
    e, n = aff_ref.shape
    bits = pltpu.bitcast(aff_ref[...], I32)
    tok = lax.broadcasted_iota(I32, (e, n), 1)

    def count(mask):
        return jnp.sum(mask.astype(F32), axis=1, keepdims=True)

    def value_bit(i, thr):
        cand = thr | (jnp.int32(1) << (30 - i))
        return jnp.where(count(bits >= cand) >= cap, cand, thr)

    thr = lax.fori_loop(0, 31, value_bit, jnp.zeros((e, 1), I32))
    gt = bits > thr
    eq = bits == thr
    need = cap - count(gt)

    def index_bit(i, j):
        cand = j | (jnp.int32(1) << (n.bit_length() - 1 - i))
        return jnp.where(count(eq & (tok < cand)) < need, cand, j)

    last = lax.fori_loop(0, n.bit_length(), index_bit, jnp.zeros((e, 1), I32))
    sel = gt | (eq & (tok <= last))
    sel_ref[...] = sel.astype(F32)

    ck = CUMSUM_CHUNK
    triu = (lax.broadcasted_iota(I32, (ck, ck), 0) < lax.broadcasted_iota(I32, (ck, ck), 1)).astype(BF16)

    def cumsum_chunk(c, carry):
        c0 = pl.multiple_of(c * ck, ck)
        s = sel_ref[:, pl.ds(c0, ck)]
        excl = jnp.dot(s.astype(BF16), triu, preferred_element_type=F32) + carry
        pos_ref[:, pl.ds(c0, ck)] = excl.astype(I32)
        return carry + jnp.sum(s, axis=1, keepdims=True)

    lax.fori_loop(0, n // ck, cumsum_chunk, jnp.zeros((e, 1), F32))
    posm_ref[...] = jnp.where(sel_ref[...] > 0.5, pos_ref[...], -1)


def _route_compact_body(cs_ref, posm_ref, aff_ref, idx_ref, gate_ref, acci_ref, accg_ref):
    n, e = posm_ref.shape
    rc = ROUTE_CHUNK
    acci_ref[...] = jnp.zeros_like(acci_ref)
    accg_ref[...] = jnp.zeros_like(accg_ref)
    lane = lax.broadcasted_iota(I32, (rc, LANES), 1)
    row = lax.broadcasted_iota(I32, (rc, LANES), 0)

    def chunk(c, carry):
        r0 = pl.multiple_of(c * rc, rc)
        pm = posm_ref[pl.ds(r0, rc), :]
        af = aff_ref[pl.ds(r0, rc), :]
        tok = (r0 + row).astype(F32)
        for ex in range(e):
            pe = jnp.broadcast_to(pm[:, ex:ex + 1], (rc, LANES))
            ge = jnp.broadcast_to(af[:, ex:ex + 1], (rc, LANES))
            kb = cs_ref[c * e + ex] // LANES
            for half in range(2):
                s0 = pl.multiple_of((kb + half) * LANES, LANES)
                hit = pe == (s0 + lane)
                ci = jnp.where(hit, tok, 0.0).reshape(rc // SUBLANES, SUBLANES, LANES).sum(axis=0)
                cg = jnp.where(hit, ge, 0.0).reshape(rc // SUBLANES, SUBLANES, LANES).sum(axis=0)
                acci_ref[ex, :, pl.ds(s0, LANES)] += ci
                accg_ref[ex, :, pl.ds(s0, LANES)] += cg
        return carry

    lax.fori_loop(0, n // rc, chunk, 0)
    idx_ref[...] = jnp.sum(acci_ref[...], axis=1).astype(I32)
    gate_ref[...] = jnp.sum(accg_ref[...], axis=1)


def route_select(aff, aff_t, off, n_tok, cap):
    e = aff.shape[1]
    assert off % n_tok == 0 and n_tok % CUMSUM_CHUNK == 0 and cap % LANES == 0
    lane_dense = pl.BlockSpec((e, n_tok), lambda i: (0, 0))
    posm_t, pos_t = pl.pallas_call(
        functools.partial(_route_pos_body, cap=cap),
        grid=(1,),
        in_specs=[pl.BlockSpec((e, n_tok), lambda i: (0, off // n_tok))],
        out_specs=[lane_dense, lane_dense],
        out_shape=[jax.ShapeDtypeStruct((e, n_tok), I32), jax.ShapeDtypeStruct((e, n_tok), I32)],
        scratch_shapes=[pltpu.VMEM((e, n_tok), F32)],
        compiler_params=_cparams("arbitrary"),
        name="route_pos",
    )(aff_t)
    posm = posm_t.T
    cs = pos_t[:, ::ROUTE_CHUNK].T
    cpad = cap + 2 * LANES
    idx, gate = pl.pallas_call(
        _route_compact_body,
        grid_spec=pltpu.PrefetchScalarGridSpec(
            num_scalar_prefetch=1,
            grid=(1,),
            in_specs=[pl.BlockSpec((n_tok, e), lambda i, cs: (0, 0)),
                      pl.BlockSpec((n_tok, e), lambda i, cs: (off // n_tok, 0))],
            out_specs=[pl.BlockSpec((e, cpad), lambda i, cs: (0, 0)), pl.BlockSpec((e, cpad), lambda i, cs: (0, 0))],
            scratch_shapes=[pltpu.VMEM((e, SUBLANES, cpad), F32), pltpu.VMEM((e, SUBLANES, cpad), F32)],
        ),
        out_shape=[jax.ShapeDtypeStruct((e, cpad), I32), jax.ShapeDtypeStruct((e, cpad), F32)],
        compiler_params=_cparams("arbitrary"),
        name="route_compact",
    )(cs.reshape(-1), posm, aff)
    return idx[:, :cap], gate[:, :cap]


FFN_COL_CHUNK = 256
FFN_ROW_CHUNK = 512
FFN_GATHER_UNROLL = 32


def _unpack_bf16_pair(u):
    lo = pltpu.bitcast(u << 16, F32).astype(BF16)
    hi = pltpu.bitcast(u & jnp.uint32(0xFFFF0000), F32).astype(BF16)
    return lo, hi


def _ffn_body(idx_ref, hp_hbm, gate_ref, wg_ref, wu_ref, wd_ref, ye_ref, xg_ref, hid_ref, sem, *, rows, nf):
    ex = pl.program_id(0)
    s = pl.program_id(1)
    half = wg_ref.shape[0] // 2
    fc = wg_ref.shape[1]
    rc = FFN_ROW_CHUNK

    @pl.when(s == 0)
    def _gather():
        def issue(g, carry):
            g0 = pl.multiple_of(g * FFN_GATHER_UNROLL, FFN_GATHER_UNROLL)
            dst = xg_ref.at[pl.ds(g0, FFN_GATHER_UNROLL)]
            for u in range(FFN_GATHER_UNROLL):
                t = idx_ref[ex * rows + g0 + u]
                pltpu.make_async_copy(hp_hbm.at[pl.ds(t, 1)], dst.at[pl.ds(u, 1)], sem).start()
            return carry
        lax.fori_loop(0, rows // FFN_GATHER_UNROLL, issue, 0)
        pltpu.make_async_copy(hp_hbm.at[pl.ds(0, rows)], xg_ref, sem).wait()

    @pl.when(s < nf)
    def _gate_up():
        wg_lo, wg_hi = wg_ref[:half, :].astype(BF16), wg_ref[half:, :].astype(BF16)
        wu_lo, wu_hi = wu_ref[:half, :].astype(BF16), wu_ref[half:, :].astype(BF16)
        col = pl.multiple_of(s * fc, fc)
        for r in range(rows // rc):
            x_lo, x_hi = _unpack_bf16_pair(xg_ref[r * rc:(r + 1) * rc, :])
            g = (jnp.dot(x_lo, wg_lo, preferred_element_type=F32)
                 + jnp.dot(x_hi, wg_hi, preferred_element_type=F32))
            u = (jnp.dot(x_lo, wu_lo, preferred_element_type=F32)
                 + jnp.dot(x_hi, wu_hi, preferred_element_type=F32))
            hid_ref[r * rc:(r + 1) * rc, pl.ds(col, fc)] = (g * jax.nn.sigmoid(g) * u).astype(BF16)

    @pl.when(s >= nf)
    def _down():
        wd = wd_ref[...].astype(BF16)
        for r in range(rows // rc):
            y = jnp.dot(hid_ref[r * rc:(r + 1) * rc, :], wd, preferred_element_type=F32)
            ye_ref[r * rc:(r + 1) * rc, :] = y * gate_ref[r * rc:(r + 1) * rc, :]


def expert_ffn(idx_flat, gate_col, hp, w_gate, w_up, w_down, layer, rows):
    n_e, d, f = w_gate.shape[1:]
    cc = FFN_COL_CHUNK
    nf, nd = f // cc, d // cc
    assert rows % FFN_ROW_CHUNK == 0
    wspec_up = pl.BlockSpec((None, None, d, cc), lambda ex, s, idx: (layer, ex, 0, jnp.minimum(s, nf - 1)))
    wspec_dn = pl.BlockSpec((None, None, f, cc), lambda ex, s, idx: (layer, ex, 0, jnp.maximum(s - nf, 0)))
    return pl.pallas_call(
        functools.partial(_ffn_body, rows=rows, nf=nf),
        grid_spec=pltpu.PrefetchScalarGridSpec(
            num_scalar_prefetch=1,
            grid=(n_e, nf + nd),
            in_specs=[pl.BlockSpec(memory_space=pl.ANY),
                      pl.BlockSpec((None, rows, 1), lambda ex, s, idx: (ex, 0, 0)),
                      wspec_up, wspec_up, wspec_dn],
            out_specs=pl.BlockSpec((None, rows, cc), lambda ex, s, idx: (ex, 0, jnp.maximum(s - nf, 0))),
            scratch_shapes=[pltpu.VMEM((rows, d // 2), U32), pltpu.VMEM((rows, f), BF16),
                            pltpu.SemaphoreType.DMA],
        ),
        out_shape=jax.ShapeDtypeStruct((n_e, rows, d), F32),
        compiler_params=_cparams("arbitrary", "arbitrary"),
        name="expert_ffn",
    )(idx_flat, hp, gate_col, w_gate, w_up, w_down)


COMBINE_ROWS = 256
COMBINE_SLOTS = 3
COMBINE_MAX_TILES = 6


def _combine_body(idx_ref, ye_ref, x_in_hbm, x_hbm, buf_ref, sem_in, sem_out, *, rows, tiles):
    del x_in_hbm
    tm = COMBINE_ROWS
    base = pl.program_id(0) * rows + pl.program_id(1) * (tiles * tm)

    def gather(k):
        s = k % COMBINE_SLOTS
        for i in range(tm):
            t = idx_ref[base + k * tm + i]
            pltpu.make_async_copy(x_hbm.at[pl.ds(t, 1)], buf_ref.at[s, pl.ds(i, 1)], sem_in.at[s]).start()

    def scatter(k):
        s = k % COMBINE_SLOTS
        for i in range(tm):
            t = idx_ref[base + k * tm + i]
            pltpu.make_async_copy(buf_ref.at[s, pl.ds(i, 1)], x_hbm.at[pl.ds(t, 1)], sem_out.at[s]).start()

    def wait_gather(k):
        s = k % COMBINE_SLOTS
        pltpu.make_async_copy(x_hbm.at[pl.ds(0, tm)], buf_ref.at[s], sem_in.at[s]).wait()

    def wait_scatter(k):
        s = k % COMBINE_SLOTS
        pltpu.make_async_copy(buf_ref.at[s], x_hbm.at[pl.ds(0, tm)], sem_out.at[s]).wait()

    ahead = COMBINE_SLOTS - 1
    for k in range(min(ahead, tiles)):
        gather(k)
    drained = 0
    for k in range(tiles):
        s = k % COMBINE_SLOTS
        wait_gather(k)
        buf_ref[s] = buf_ref[s] + ye_ref[k * tm:(k + 1) * tm, :]
        scatter(k)
        if k + ahead < tiles:
            if k >= 1:
                wait_scatter(k - 1)
                drained = k
            gather(k + ahead)
    for k in range(drained, tiles):
        wait_scatter(k)


def combine(idx_flat, ye, x):
    n_e, rows, d = ye.shape
    tm = COMBINE_ROWS
    n_tiles = rows // tm
    assert rows % tm == 0
    tiles = max(c for c in range(1, COMBINE_MAX_TILES + 1) if n_tiles % c == 0)
    return pl.pallas_call(
        functools.partial(_combine_body, rows=rows, tiles=tiles),
        grid_spec=pltpu.PrefetchScalarGridSpec(
            num_scalar_prefetch=1,
            grid=(n_e, n_tiles // tiles),
            in_specs=[pl.BlockSpec((None, tiles * tm, d), lambda ex, j, idx: (ex, j, 0)),
                      pl.BlockSpec(memory_space=pl.ANY)],
            out_specs=pl.BlockSpec(memory_space=pl.ANY),
            scratch_shapes=[pltpu.VMEM((COMBINE_SLOTS, tm, d), F32), pltpu.SemaphoreType.DMA((COMBINE_SLOTS,)),
                            pltpu.SemaphoreType.DMA((COMBINE_SLOTS,))],
        ),
        out_shape=jax.ShapeDtypeStruct(x.shape, x.dtype),
        input_output_aliases={2: 0},
        compiler_params=_cparams("arbitrary", "arbitrary"),
        name="combine",
    )(idx_flat, ye, x)


def _rope_tables(segments, n):
    half = MLA_ROPE_DIM // 2
    inv = jnp.power(ROPE_THETA, -jnp.arange(0, MLA_ROPE_DIM, 2, dtype=F32) / MLA_ROPE_DIM)
    pos = jnp.concatenate([jnp.tile(jnp.arange(t, dtype=F32), n_seq) for (_, n_seq, t) in segments])
    assert pos.shape[0] == n

    def cs(p):
        ang = p[:, None] * inv[None, :]
        return jnp.cos(ang), jnp.sin(ang)

    c, s = cs(pos)
    z = jnp.zeros((n, 2 * half), F32)
    cos_a = jnp.concatenate([c, c, z], axis=1)
    sin_a = jnp.concatenate([-s, s, z], axis=1)
    cr, sr = cs(jnp.floor(pos / GRID_W))
    cc, sc = cs(pos - GRID_W * jnp.floor(pos / GRID_W))
    cos_b = jnp.concatenate([cr, cr, cc, cc], axis=1)
    sin_b = jnp.concatenate([-sr, sr, -sc, sc], axis=1)
    return cos_a, sin_a, cos_b, sin_b


def _pad_cols(w, width):
    return jnp.pad(w, ((0, 0), (0, width - w.shape[1])))


def _even_layer_weights(w_in, w_uq, w_ukv, q_gain, k_gain):
    o = [0, MLA_Q_RANK, MLA_Q_RANK + MLA_KV_RANK]
    o.append(o[-1] + MLA_ROPE_DIM)
    o.append(o[-1] + GQA_Q_HEADS * GQA_HEAD_DIM)
    o.append(o[-1] + GQA_KV_HEADS * GQA_HEAD_DIM)
    o.append(o[-1] + GQA_KV_HEADS * GQA_HEAD_DIM)
    pieces = [w_in[:, o[0]:o[2]], _pad_cols(w_in[:, o[2]:o[3]], LANES), w_in[:, o[3]:o[6]]]
    w_in_p = _pad_cols(jnp.concatenate(pieces, axis=1), AB_WIDTH).astype(BF16)
    qd = MLA_NOPE_DIM + MLA_ROPE_DIM
    w_uq_p = jnp.pad(w_uq.reshape(MLA_Q_RANK, MLA_HEADS, qd), ((0, 0), (0, 0), (0, 2 * LANES - qd)))
    w_uq_p = w_uq_p.reshape(MLA_Q_RANK, MLA_HEADS * 2 * LANES).astype(BF16)
    gq = _pad_cols(q_gain.reshape(1, qd), 2 * LANES)
    gk = _pad_cols(k_gain.reshape(1, qd), 2 * LANES)
    return w_in_p, w_uq_p, w_ukv.astype(BF16), gq, gk


def _alibi_slopes():
    return jnp.power(2.0, -8.0 * jnp.arange(1, DIL_HEADS + 1, dtype=F32) / DIL_HEADS)


def kernel(x_prompt, x_sample, ln_mix, ln_ffn, w_in_ab, mla_q_norm, mla_w_uq, mla_kv_norm, mla_w_ukv, mla_q_gain, mla_k_gain, gqa_q_gain, gqa_k_gain, w_out_ab, w_in_c, c_q_gain, c_k_gain, w_out_c, w_router, w_gate, w_up, w_down):
    d = x_prompt.shape[-1]
    groups = [x_sample, x_prompt]
    segments, off = [], 0
    for g in groups:
        segments.append((off, g.shape[0], g.shape[1]))
        off += g.shape[0] * g.shape[1]
    n = off
    x = jnp.concatenate([g.reshape(-1, d) for g in groups], axis=0)
    depth = ln_mix.shape[0]
    n_e = w_router.shape[-1]
    caps = [CAPACITY_FACTOR * n_seq * t // n_e for (_, n_seq, t) in segments]
    rows = sum(caps)
    tabs = _rope_tables(segments, n)
    slopes = _alibi_slopes()

    for layer in range(depth):
        i = layer // 2
        h = rmsnorm_rows(x, ln_mix[layer])
        if layer % 2 == 0:
            w_in_p, w_uq_p, w_ukv_b, gq, gk = _even_layer_weights(
                w_in_ab[i], mla_w_uq[i], mla_w_ukv[i], mla_q_gain[i], mla_k_gain[i])
            proj = matmul(h, w_in_p, tn=AB_WIDTH // 2)
            qa, ka, va, qb, kb, vb = mla_gqa_prep(
                proj, w_uq_p, w_ukv_b, mla_q_norm[i].reshape(1, -1), mla_kv_norm[i].reshape(1, -1), gq, gk,
                gqa_q_gain[i].reshape(1, -1), gqa_k_gain[i].reshape(1, -1), tabs)
            o_a = attention(qa, ka, va, segments, MLA_HEADS, MLA_HEADS, 2 * LANES, MLA_V_DIM)
            o_b = attention(qb, kb, vb, segments, GQA_Q_HEADS, GQA_KV_HEADS, GQA_HEAD_DIM, GQA_HEAD_DIM)
            x = matmul_parts([[a, b] for a, b in zip(o_a, o_b)], w_out_ab[i].astype(BF16), x, tn=d // 2)
        else:
            qkv = matmul_heads(h, w_in_c[i].astype(BF16), tn=w_in_c.shape[-1] // 4)
            o = dilated_attention(qkv, c_q_gain[i].reshape(1, -1), c_k_gain[i].reshape(1, -1), slopes, segments)
            x = matmul_parts([[seg] for seg in o], w_out_c[i].astype(BF16), x, tn=d // 2)

        hp, aff = router(x, ln_ffn[layer], w_router[layer])
        aff_t = aff.T
        idx_parts, gate_parts = [], []
        for (off, n_seq, t), cap in zip(segments, caps):
            idx, gate = route_select(aff, aff_t, off, n_seq * t, cap)
            idx_parts.append(idx + off)
            gate_parts.append(gate)
        idx_flat = jnp.concatenate(idx_parts, axis=1).reshape(-1)
        gate_col = jnp.concatenate(gate_parts, axis=1).reshape(n_e, rows, 1)
        ye = expert_ffn(idx_flat, gate_col, hp, w_gate, w_up, w_down, layer, rows)
        x = combine(idx_flat, ye, x)

    outs = []
    for g, (off, n_seq, t) in zip(groups, segments):
        outs.append(x[off:off + n_seq * t].reshape(g.shape))
    return (outs[1], outs[0])
```

```python
import functools
import math

import jax
import jax.numpy as jnp
from jax import lax
from jax.experimental import pallas as pl
from jax.experimental.pallas import tpu as pltpu

F32 = jnp.float32
BF16 = jnp.bfloat16
I32 = jnp.int32
U32 = jnp.uint32

D_MODEL = 2048
NORM_EPS = 1e-6
MASK_VALUE = -1e30
GRID_W = 64

MLA_HEADS = 8
MLA_Q_RANK = 512
MLA_KV_RANK = 256
MLA_NOPE_DIM = 128
MLA_ROPE_DIM = 64
MLA_V_DIM = 128
GQA_Q_HEADS = 8
GQA_KV_HEADS = 2
GQA_HEAD_DIM = 128
ROPE_THETA = 10000.0

DIL_HEADS = 16
DIL_HEAD_DIM = 128
DIL_BRANCHES = ((128, 1), (512, 4), (2048, 16))
DIL_Q_BLOCK = 128
DIL_NORM_CHUNK = 256

N_EXPERTS = 16
CAPACITY_FACTOR = 2

LOG2_E = 1.4426950408889634
LANES = 128
SUBLANES = 8
VMEM_LIMIT_BYTES = 56 * 1024 * 1024

AB_COLS = dict(cq=0, ckv=512, kpe=768, qb=896, kb=1920, vb=2176)
AB_WIDTH = 2560


def _cparams(*sem):
    return pltpu.CompilerParams(dimension_semantics=sem, vmem_limit_bytes=VMEM_LIMIT_BYTES)


def _rmsnorm_body(x_ref, g_ref, o_ref):
    x = x_ref[...]
    ms = jnp.mean(x * x, axis=-1, keepdims=True)
    o_ref[...] = (x * lax.rsqrt(ms + NORM_EPS) * g_ref[...]).astype(o_ref.dtype)


def rmsnorm_rows(x, g, tm=512):
    n, d = x.shape
    return pl.pallas_call(
        _rmsnorm_body,
        grid=(n // tm,),
        in_specs=[pl.BlockSpec((tm, d), lambda i: (i, 0)), pl.BlockSpec((1, d), lambda i: (0, 0))],
        out_specs=pl.BlockSpec((tm, d), lambda i: (i, 0)),
        out_shape=jax.ShapeDtypeStruct((n, d), BF16),
        compiler_params=_cparams("parallel"),
        name="rmsnorm_rows",
    )(x, g.reshape(1, d))


def _matmul_body(x_ref, w_ref, o_ref):
    o_ref[...] = jnp.dot(x_ref[...], w_ref[...], preferred_element_type=F32)


def matmul(x, w, tm=512, tn=None):
    n, k = x.shape
    m = w.shape[1]
    tn = m if tn is None else tn
    return pl.pallas_call(
        _matmul_body,
        grid=(m // tn, n // tm),
        in_specs=[pl.BlockSpec((tm, k), lambda j, i: (i, 0)), pl.BlockSpec((k, tn), lambda j, i: (0, j))],
        out_specs=pl.BlockSpec((tm, tn), lambda j, i: (i, j)),
        out_shape=jax.ShapeDtypeStruct((n, m), F32),
        compiler_params=_cparams("parallel", "parallel"),
        name="matmul",
    )(x, w)


def _matmul_heads_body(x_ref, w_ref, o_ref):
    acc = jnp.dot(x_ref[...], w_ref[...], preferred_element_type=F32)
    for h in range(o_ref.shape[0]):
        o_ref[h] = acc[:, h * LANES:(h + 1) * LANES]


def matmul_heads(x, w, tm=512, tn=None):
    n, k = x.shape
    m = w.shape[1]
    tn = m if tn is None else tn
    return pl.pallas_call(
        _matmul_heads_body,
        grid=(m // tn, n // tm),
        in_specs=[pl.BlockSpec((tm, k), lambda j, i: (i, 0)), pl.BlockSpec((k, tn), lambda j, i: (0, j))],
        out_specs=pl.BlockSpec((tn // LANES, tm, LANES), lambda j, i: (j, i, 0)),
        out_shape=jax.ShapeDtypeStruct((m // LANES, n, LANES), F32),
        compiler_params=_cparams("parallel", "parallel"),
        name="matmul_heads",
    )(x, w)


def _matmul_parts_body(*refs, n_seg, n_part, seg_tiles):
    x_refs = refs[:n_seg * n_part]
    w_ref, r_ref, o_ref = refs[n_seg * n_part:]
    i = pl.program_id(1)
    start = 0
    for s in range(n_seg):
        @pl.when((i >= start) & (i < start + seg_tiles[s]))
        def _segment(s=s):
            acc = r_ref[...]
            k0 = 0
            for p in range(n_part):
                x_ref = x_refs[s * n_part + p]
                if len(x_ref.shape) == 3:
                    x = jnp.concatenate([x_ref[h] for h in range(x_ref.shape[0])], axis=1)
                else:
                    x = x_ref[...]
                acc = acc + jnp.dot(x, w_ref[k0:k0 + x.shape[1], :], preferred_element_type=F32)
                k0 += x.shape[1]
            o_ref[...] = acc
        start += seg_tiles[s]


def matmul_parts(parts, w, res, tm=512, tn=None):
    n, m = res.shape
    k = w.shape[0]
    tn = m if tn is None else tn
    n_seg, n_part = len(parts), len(parts[0])
    seg_tiles = [seg[0].shape[-2] // tm for seg in parts]
    assert sum(seg_tiles) * tm == n and all(seg[0].shape[-2] % tm == 0 for seg in parts)
    in_specs, args, start = [], [], 0
    for s, seg in enumerate(parts):
        for x in seg:
            row = lambda i, start=start, last=seg_tiles[s] - 1: jnp.clip(i - start, 0, last)
            if x.ndim == 3:
                in_specs.append(pl.BlockSpec((x.shape[0], tm, x.shape[2]), lambda j, i, row=row: (0, row(i), 0)))
            else:
                in_specs.append(pl.BlockSpec((tm, x.shape[1]), lambda j, i, row=row: (row(i), 0)))
            args.append(x)
        start += seg_tiles[s]
    in_specs += [pl.BlockSpec((k, tn), lambda j, i: (0, j)), pl.BlockSpec((tm, tn), lambda j, i: (i, j))]
    return pl.pallas_call(
        functools.partial(_matmul_parts_body, n_seg=n_seg, n_part=n_part, seg_tiles=tuple(seg_tiles)),
        grid=(m // tn, n // tm),
        in_specs=in_specs,
        out_specs=pl.BlockSpec((tm, tn), lambda j, i: (i, j)),
        out_shape=jax.ShapeDtypeStruct((n, m), F32),
        compiler_params=_cparams("parallel", "arbitrary"),
        name="matmul_parts",
    )(*args, w, res)


def _rms(x, g, n):
    return x * lax.rsqrt(jnp.sum(x * x, axis=-1, keepdims=True) * (1.0 / n) + NORM_EPS) * g


def _rotate_pairs(x, cos, sin):
    lane = lax.broadcasted_iota(I32, x.shape, 1)
    swapped = jnp.where((lane % 64) < 32, pltpu.roll(x, 96, 1), pltpu.roll(x, 32, 1))
    return x * cos + swapped * sin


def _prep_body(p_ref, wuq_ref, wukv_ref, qn_ref, kvn_ref, gq_ref, gk_ref, gbq_ref, gbk_ref,
               ca_ref, sa_ref, cb_ref, sb_ref,
               qa_ref, ka_ref, va_ref, qb_ref, kb_ref, vb_ref):
    c = AB_COLS
    scale_a = LOG2_E / math.sqrt(MLA_NOPE_DIM + MLA_ROPE_DIM)
    scale_b = LOG2_E / math.sqrt(GQA_HEAD_DIM)
    ca, sa, cb, sb = ca_ref[...], sa_ref[...], cb_ref[...], sb_ref[...]
    gq, gk = gq_ref[...], gk_ref[...]

    cqn = _rms(p_ref[:, c["cq"]:c["cq"] + MLA_Q_RANK], qn_ref[...], MLA_Q_RANK).astype(BF16)
    q = jnp.dot(cqn, wuq_ref[...], preferred_element_type=F32)
    ckvn = _rms(p_ref[:, c["ckv"]:c["ckv"] + MLA_KV_RANK], kvn_ref[...], MLA_KV_RANK).astype(BF16)
    kv = jnp.dot(ckvn, wukv_ref[...], preferred_element_type=F32)

    kpe = p_ref[:, c["kpe"]:c["kpe"] + LANES]
    kpe = _rotate_pairs(_rms(kpe, gk[:, LANES:], MLA_ROPE_DIM), ca, sa).astype(BF16)

    for h in range(MLA_HEADS):
        o = 2 * LANES * h
        qn = _rms(q[:, o:o + LANES], gq[:, :LANES], MLA_NOPE_DIM) * scale_a
        qr = _rotate_pairs(_rms(q[:, o + LANES:o + 2 * LANES], gq[:, LANES:], MLA_ROPE_DIM), ca, sa) * scale_a
        qa_ref[:, o:o + LANES] = qn.astype(BF16)
        qa_ref[:, o + LANES:o + 2 * LANES] = qr.astype(BF16)
        ka_ref[:, o:o + LANES] = _rms(kv[:, o:o + LANES], gk[:, :LANES], MLA_NOPE_DIM).astype(BF16)
        ka_ref[:, o + LANES:o + 2 * LANES] = kpe
        va_ref[:, LANES * h:LANES * (h + 1)] = kv[:, o + LANES:o + 2 * LANES].astype(BF16)

    for h in range(GQA_Q_HEADS):
        x = p_ref[:, c["qb"] + LANES * h:c["qb"] + LANES * (h + 1)]
        x = _rotate_pairs(_rms(x, gbq_ref[...], GQA_HEAD_DIM), cb, sb) * scale_b
        qb_ref[:, LANES * h:LANES * (h + 1)] = x.astype(BF16)
    for h in range(GQA_KV_HEADS):
        x = p_ref[:, c["kb"] + LANES * h:c["kb"] + LANES * (h + 1)]
        x = _rotate_pairs(_rms(x, gbk_ref[...], GQA_HEAD_DIM), cb, sb)
        kb_ref[:, LANES * h:LANES * (h + 1)] = x.astype(BF16)
        vb_ref[:, LANES * h:LANES * (h + 1)] = p_ref[:, c["vb"] + LANES * h:c["vb"] + LANES * (h + 1)].astype(BF16)


def mla_gqa_prep(proj, wuq, wukv, qn, kvn, gq, gk, gbq, gbk, tabs, tm=256):
    n = proj.shape[0]
    row = lambda w: pl.BlockSpec((tm, w), lambda i: (i, 0))
    full = lambda a: pl.BlockSpec(a.shape, lambda i: (0, 0))
    small = [wuq, wukv, qn, kvn, gq, gk, gbq, gbk]
    widths = [2 * LANES * MLA_HEADS, 2 * LANES * MLA_HEADS, LANES * MLA_HEADS,
              LANES * GQA_Q_HEADS, LANES * GQA_KV_HEADS, LANES * GQA_KV_HEADS]
    return pl.pallas_call(
        _prep_body,
        grid=(n // tm,),
        in_specs=[row(AB_WIDTH)] + [full(a) for a in small] + [row(LANES)] * 4,
        out_specs=[row(w) for w in widths],
        out_shape=[jax.ShapeDtypeStruct((n, w), BF16) for w in widths],
        compiler_params=_cparams("parallel"),
        name="mla_gqa_prep",
    )(proj, *small, *tabs)


def _attn_body(q_ref, k_ref, v_ref, o_ref, *, tk):
    q = q_ref[...]
    tq = q.shape[0]
    dv = v_ref.shape[1]
    n_kv = k_ref.shape[0] // tk

    def step(j, carry):
        m, l, acc = carry
        r0 = pl.multiple_of(j * tk, tk)
        k = k_ref[pl.ds(r0, tk), :]
        v = v_ref[pl.ds(r0, tk), :]
        s = lax.dot_general(q, k, (((1,), (1,)), ((), ())), preferred_element_type=F32)
        m_new = jnp.maximum(m, jnp.max(s, axis=-1, keepdims=True))
        alpha = jnp.exp2(m - m_new)
        p = jnp.exp2(s - m_new)
        l = alpha * l + jnp.sum(p, axis=-1, keepdims=True)
        acc = alpha * acc + jnp.dot(p.astype(BF16), v, preferred_element_type=F32)
        return m_new, l, acc

    init = (jnp.full((tq, 1), -jnp.inf, F32), jnp.zeros((tq, 1), F32), jnp.zeros((tq, dv), F32))
    m, l, acc = lax.fori_loop(0, n_kv, step, init)
    o_ref[...] = (acc / l).astype(o_ref.dtype)


def attention(q, k, v, segments, hq, hk, dk, dv, tq=2048, tk=2048):
    g = hq // hk
    outs = []
    for (off, n_seq, t) in segments:
        tq_s, tk_s = min(tq, t), min(tk, t)
        assert off % t == 0 and t % tq_s == 0 and t % tk_s == 0
        qblk = lambda b, h, i, off=off, t=t, tq_s=tq_s: ((off + b * t) // tq_s + i, h)
        oblk = lambda b, h, i, t=t, tq_s=tq_s: ((b * t) // tq_s + i, h)
        kvblk = lambda b, h, i, off=off, t=t: (off // t + b, h // g)
        outs.append(pl.pallas_call(
            functools.partial(_attn_body, tk=tk_s),
            grid=(n_seq, hq, t // tq_s),
            in_specs=[pl.BlockSpec((tq_s, dk), qblk), pl.BlockSpec((t, dk), kvblk), pl.BlockSpec((t, dv), kvblk)],
            out_specs=pl.BlockSpec((tq_s, dv), oblk),
            out_shape=jax.ShapeDtypeStruct((n_seq * t, hq * dv), BF16),
            compiler_params=_cparams("parallel", "parallel", "arbitrary"),
            name="attention",
        )(q, k, v))
    return outs


def _dilated_body(slope_ref, q_ref, k_ref, v_ref, gq_ref, gk_ref, o_ref, qn_ref, kn_ref, bias_ref, ob_ref, lb_ref,
                  *, t, tile):
    h = pl.program_id(1)
    slope = slope_ref[h]
    gq, gk = gq_ref[...], gk_ref[...]
    scale = 1.0 / math.sqrt(DIL_HEAD_DIM)
    qb = DIL_Q_BLOCK
    wmax = bias_ref.shape[-1]

    def norm_chunk(c, carry):
        rows = pl.ds(pl.multiple_of(c * DIL_NORM_CHUNK, DIL_NORM_CHUNK), DIL_NORM_CHUNK)
        qn_ref[rows, :] = _rms(q_ref[rows, :], gq, DIL_HEAD_DIM) * scale
        kn_ref[rows, :] = _rms(k_ref[rows, :], gk, DIL_HEAD_DIM)
        return carry

    lax.fori_loop(0, t // DIL_NORM_CHUNK, norm_chunk, 0)

    rel0 = lax.broadcasted_iota(I32, (qb, wmax), 1) - lax.broadcasted_iota(I32, (qb, wmax), 0)
    for bi, (window, d) in enumerate(DIL_BRANCHES):
        nh = (window // 2) // d
        for var in range(3):
            dist = jnp.abs(rel0 - nh * var)
            bias_ref[bi, var] = jnp.where(dist <= nh, -slope * (d * dist).astype(F32), MASK_VALUE)

    def tile_body(ti, carry):
        base = pl.multiple_of(ti * tile, tile)
        for bi, (window, d) in enumerate(DIL_BRANCHES):
            nh = (window // 2) // d
            ln = t // d
            w = min(qb + 2 * nh, ln)
            for r in range(d):
                for bb in range(tile // (qb * d)):
                    p0 = ti * (tile // d) + qb * bb
                    ws = jnp.clip(p0 - nh, 0, ln - w)
                    qn = qn_ref[pl.ds(base + r + d * qb * bb, qb, stride=d), :].astype(BF16)
                    kn = kn_ref[pl.ds(r + d * ws, w, stride=d), :].astype(BF16)
                    vw = v_ref[pl.ds(r + d * ws, w, stride=d), :]
                    s = lax.dot_general(qn, kn, (((1,), (1,)), ((), ())), preferred_element_type=F32)
                    s = s + bias_ref[bi, (p0 - ws) // nh, :, :w]
                    m = jnp.max(s, axis=-1, keepdims=True)
                    p = jnp.exp(s - m).astype(BF16)
                    v1 = jnp.concatenate([vw.astype(BF16), jnp.ones((w, DIL_HEAD_DIM), BF16)], axis=1)
                    od = jnp.dot(p, v1, preferred_element_type=F32)
                    den = od[:, DIL_HEAD_DIM:]
                    rows = pl.ds(r + d * qb * bb, qb, stride=d)
                    ob_ref[bi, rows, :] = od[:, :DIL_HEAD_DIM] / den
                    lb_ref[bi, rows, :] = m + jnp.log(den)
        l0, l1, l2 = lb_ref[0], lb_ref[1], lb_ref[2]
        mx = jnp.maximum(jnp.maximum(l0, l1), l2)
        w0, w1, w2 = jnp.exp(l0 - mx), jnp.exp(l1 - mx), jnp.exp(l2 - mx)
        inv = 1.0 / (w0 + w1 + w2)
        merged = (w0 * inv) * ob_ref[0] + (w1 * inv) * ob_ref[1] + (w2 * inv) * ob_ref[2]
        o_ref[pl.ds(base, tile), :] = merged.astype(o_ref.dtype)
        return carry

    lax.fori_loop(0, t // tile, tile_body, 0)


def dilated_attention(qkv, gq, gk, slopes, segments):
    hd = DIL_HEAD_DIM
    max_d = max(d for _, d in DIL_BRANCHES)
    tile = DIL_Q_BLOCK * max_d
    max_nh = max((window // 2) // d for window, d in DIL_BRANCHES)
    assert max_nh <= DIL_Q_BLOCK
    wmax = DIL_Q_BLOCK + 2 * max_nh
    n_br = len(DIL_BRANCHES)
    outs = []
    for (off, n_seq, t) in segments:
        assert off % t == 0 and t % tile == 0 and t % DIL_NORM_CHUNK == 0
        blk = lambda c0: pl.BlockSpec((None, t, hd), lambda b, h, s, off=off, t=t, c0=c0: (c0 + h, off // t + b, 0))
        vec = pl.BlockSpec((1, hd), lambda b, h, s: (0, 0))
        outs.append(pl.pallas_call(
            functools.partial(_dilated_body, t=t, tile=tile),
            grid_spec=pltpu.PrefetchScalarGridSpec(
                num_scalar_prefetch=1,
                grid=(n_seq, DIL_HEADS),
                in_specs=[blk(0), blk(DIL_HEADS), blk(2 * DIL_HEADS), vec, vec],
                out_specs=pl.BlockSpec((None, t, hd), lambda b, h, s: (h, b, 0)),
                scratch_shapes=[pltpu.VMEM((t, hd), F32), pltpu.VMEM((t, hd), F32),
                                pltpu.VMEM((n_br, 3, DIL_Q_BLOCK, wmax), F32),
                                pltpu.VMEM((n_br, tile, hd), F32), pltpu.VMEM((n_br, tile, hd), F32)],
            ),
            out_shape=jax.ShapeDtypeStruct((DIL_HEADS, n_seq * t, hd), BF16),
            compiler_params=_cparams("parallel", "arbitrary"),
            name="dilated_attention",
        )(slopes, qkv, qkv, qkv, gq, gk))
    return outs


def _split_bf16(x):
    hi = x.astype(BF16)
    lo = (x - hi.astype(F32)).astype(BF16)
    return hi, lo


def _router_body(x_ref, g_ref, w_ref, hp_ref, aff_ref):
    x = x_ref[...]
    d = x.shape[1]
    ms = jnp.mean(x * x, axis=-1, keepdims=True)
    hn = x * lax.rsqrt(ms + NORM_EPS) * g_ref[...]
    xh, xl = _split_bf16(hn)
    wh, wl = _split_bf16(w_ref[...])
    logits = (jnp.dot(xh, wh, preferred_element_type=F32) + jnp.dot(xl, wh, preferred_element_type=F32)
              + jnp.dot(xh, wl, preferred_element_type=F32))
    mx = jnp.max(logits, axis=-1, keepdims=True)
    ex = jnp.exp(logits - mx)
    aff_ref[...] = ex / jnp.sum(ex, axis=-1, keepdims=True)
    r = pltpu.bitcast(xh.astype(F32), U32)
    hp_ref[...] = (r[:, :d // 2] >> 16) | (r[:, d // 2:] & jnp.uint32(0xFFFF0000))


def router(x, g, w_router, tm=512):
    n, d = x.shape
    e = w_router.shape[1]
    return pl.pallas_call(
        _router_body,
        grid=(n // tm,),
        in_specs=[pl.BlockSpec((tm, d), lambda i: (i, 0)), pl.BlockSpec((1, d), lambda i: (0, 0)),
                  pl.BlockSpec((d, e), lambda i: (0, 0))],
        out_specs=[pl.BlockSpec((tm, d // 2), lambda i: (i, 0)), pl.BlockSpec((tm, e), lambda i: (i, 0))],
        out_shape=[jax.ShapeDtypeStruct((n, d // 2), U32), jax.ShapeDtypeStruct((n, e), F32)],
        compiler_params=_cparams("parallel"),
        name="router",
    )(x, g.reshape(1, d), w_router)


ROUTE_CHUNK = 128
CUMSUM_CHUNK = 256


def _route_pos_body(aff_ref, posm_ref, pos_ref, sel_ref, *, cap):
    e, n = aff_ref.shape
    bits = pltpu.bitcast(aff_ref[...], I32)
    tok = lax.broadcasted_iota(I32, (e, n), 1)

    def count(mask):
        return jnp.sum(mask.astype(F32), axis=1, keepdims=True)

    def value_bit(i, thr):
        cand = thr | (jnp.int32(1) << (30 - i))
        return jnp.where(count(bits >= cand) >= cap, cand, thr)

    thr = lax.fori_loop(0, 31, value_bit, jnp.zeros((e, 1), I32))
    gt = bits > thr
    eq = bits == thr
    need = cap - count(gt)

    def index_bit(i, j):
        cand = j | (jnp.int32(1) << (n.bit_length() - 1 - i))
        return jnp.where(count(eq & (tok < cand)) < need, cand, j)

    last = lax.fori_loop(0, n.bit_length(), index_bit, jnp.zeros((e, 1), I32))
    sel = gt | (eq & (tok <= last))
    sel_ref[...] = sel.astype(F32)

    ck = CUMSUM_CHUNK
    triu = (lax.broadcasted_iota(I32, (ck, ck), 0) < lax.broadcasted_iota(I32, (ck, ck), 1)).astype(BF16)

    def cumsum_chunk(c, carry):
        c0 = pl.multiple_of(c * ck, ck)
        s = sel_ref[:, pl.ds(c0, ck)]
        excl = jnp.dot(s.astype(BF16), triu, preferred_element_type=F32) + carry
        pos_ref[:, pl.ds(c0, ck)] = excl.astype(I32)
        return carry + jnp.sum(s, axis=1, keepdims=True)

    lax.fori_loop(0, n // ck, cumsum_chunk, jnp.zeros((e, 1), F32))
    posm_ref[...] = jnp.where(sel_ref[...] > 0.5, pos_ref[...], -1)


def _route_compact_body(cs_ref, posm_ref, aff_ref, idx_ref, gate_ref, acci_ref, accg_ref):
    n, e = posm_ref.shape
    rc = ROUTE_CHUNK
    acci_ref[...] = jnp.zeros_like(acci_ref)
    accg_ref[...] = jnp.zeros_like(accg_ref)
    lane = lax.broadcasted_iota(I32, (rc, LANES), 1)
    row = lax.broadcasted_iota(I32, (rc, LANES), 0)

    def chunk(c, carry):
        r0 = pl.multiple_of(c * rc, rc)
        pm = posm_ref[pl.ds(r0, rc), :]
        af = aff_ref[pl.ds(r0, rc), :]
        tok = (r0 + row).astype(F32)
        for ex in range(e):
            pe = jnp.broadcast_to(pm[:, ex:ex + 1], (rc, LANES))
            ge = jnp.broadcast_to(af[:, ex:ex + 1], (rc, LANES))
            kb = cs_ref[c * e + ex] // LANES
            for half in range(2):
                s0 = pl.multiple_of((kb + half) * LANES, LANES)
                hit = pe == (s0 + lane)
                ci = jnp.where(hit, tok, 0.0).reshape(rc // SUBLANES, SUBLANES, LANES).sum(axis=0)
                cg = jnp.where(hit, ge, 0.0).reshape(rc // SUBLANES, SUBLANES, LANES).sum(axis=0)
                acci_ref[ex, :, pl.ds(s0, LANES)] += ci
                accg_ref[ex, :, pl.ds(s0, LANES)] += cg
        return carry

    lax.fori_loop(0, n // rc, chunk, 0)
    idx_ref[...] = jnp.sum(acci_ref[...], axis=1).astype(I32)
    gate_ref[...] = jnp.sum(accg_ref[...], axis=1)


def route_select(aff, aff_t, off, n_tok, cap):
    e = aff.shape[1]
    assert off % n_tok == 0 and n_tok % CUMSUM_CHUNK == 0 and cap % LANES == 0
    lane_dense = pl.BlockSpec((e, n_tok), lambda i: (0, 0))
    posm_t, pos_t = pl.pallas_call(
        functools.partial(_route_pos_body, cap=cap),
        grid=(1,),
        in_specs=[pl.BlockSpec((e, n_tok), lambda i: (0, off // n_tok))],
        out_specs=[lane_dense, lane_dense],
        out_shape=[jax.ShapeDtypeStruct((e, n_tok), I32), jax.ShapeDtypeStruct((e, n_tok), I32)],
        scratch_shapes=[pltpu.VMEM((e, n_tok), F32)],
        compiler_params=_cparams("arbitrary"),
        name="route_pos",
    )(aff_t)
    posm = posm_t.T
    cs = pos_t[:, ::ROUTE_CHUNK].T
    cpad = cap + 2 * LANES
    idx, gate = pl.pallas_call(
        _route_compact_body,
        grid_spec=pltpu.PrefetchScalarGridSpec(
            num_scalar_prefetch=1,
            grid=(1,),
            in_specs=[pl.BlockSpec((n_tok, e), lambda i, cs: (0, 0)),
                      pl.BlockSpec((n_tok, e), lambda i, cs: (off // n_tok, 0))],
            out_specs=[pl.BlockSpec((e, cpad), lambda i, cs: (0, 0)), pl.BlockSpec((e, cpad), lambda i, cs: (0, 0))],
            scratch_shapes=[pltpu.VMEM((e, SUBLANES, cpad), F32), pltpu.VMEM((e, SUBLANES, cpad), F32)],
        ),
        out_shape=[jax.ShapeDtypeStruct((e, cpad), I32), jax.ShapeDtypeStruct((e, cpad), F32)],
        compiler_params=_cparams("arbitrary"),
        name="route_compact",
    )(cs.reshape(-1), posm, aff)
    return idx[:, :cap], gate[:, :cap]


FFN_COL_CHUNK = 256
FFN_ROW_CHUNK = 512
FFN_GATHER_UNROLL = 32


def _unpack_bf16_pair(u):
    lo = pltpu.bitcast(u << 16, F32).astype(BF16)
    hi = pltpu.bitcast(u & jnp.uint32(0xFFFF0000), F32).astype(BF16)
    return lo, hi


def _ffn_body(idx_ref, hp_hbm, gate_ref, wg_ref, wu_ref, wd_ref, ye_ref, xg_ref, hid_ref, sem, *, rows, nf):
    ex = pl.program_id(0)
    s = pl.program_id(1)
    half = wg_ref.shape[0] // 2
    fc = wg_ref.shape[1]
    rc = FFN_ROW_CHUNK

    @pl.when(s == 0)
    def _gather():
        def issue(g, carry):
            g0 = pl.multiple_of(g * FFN_GATHER_UNROLL, FFN_GATHER_UNROLL)
            dst = xg_ref.at[pl.ds(g0, FFN_GATHER_UNROLL)]
            for u in range(FFN_GATHER_UNROLL):
                t = idx_ref[ex * rows + g0 + u]
                pltpu.make_async_copy(hp_hbm.at[pl.ds(t, 1)], dst.at[pl.ds(u, 1)], sem).start()
            return carry
        lax.fori_loop(0, rows // FFN_GATHER_UNROLL, issue, 0)
        pltpu.make_async_copy(hp_hbm.at[pl.ds(0, rows)], xg_ref, sem).wait()

    @pl.when(s < nf)
    def _gate_up():
        wg_lo, wg_hi = wg_ref[:half, :].astype(BF16), wg_ref[half:, :].astype(BF16)
        wu_lo, wu_hi = wu_ref[:half, :].astype(BF16), wu_ref[half:, :].astype(BF16)
        col = pl.multiple_of(s * fc, fc)
        for r in range(rows // rc):
            x_lo, x_hi = _unpack_bf16_pair(xg_ref[r * rc:(r + 1) * rc, :])
            g = (jnp.dot(x_lo, wg_lo, preferred_element_type=F32)
                 + jnp.dot(x_hi, wg_hi, preferred_element_type=F32))
            u = (jnp.dot(x_lo, wu_lo, preferred_element_type=F32)
                 + jnp.dot(x_hi, wu_hi, preferred_element_type=F32))
            hid_ref[r * rc:(r + 1) * rc, pl.ds(col, fc)] = (g * jax.nn.sigmoid(g) * u).astype(BF16)

    @pl.when(s >= nf)
    def _down():
        wd = wd_ref[...].astype(BF16)
        for r in range(rows // rc):
            y = jnp.dot(hid_ref[r * rc:(r + 1) * rc, :], wd, preferred_element_type=F32)
            ye_ref[r * rc:(r + 1) * rc, :] = y * gate_ref[r * rc:(r + 1) * rc, :]


def expert_ffn(idx_flat, gate_col, hp, w_gate, w_up, w_down, layer, rows):
    n_e, d, f = w_gate.shape[1:]
    cc = FFN_COL_CHUNK
    nf, nd = f // cc, d // cc
    assert rows % FFN_ROW_CHUNK == 0
    wspec_up = pl.BlockSpec((None, None, d, cc), lambda ex, s, idx: (layer, ex, 0, jnp.minimum(s, nf - 1)))
    wspec_dn = pl.BlockSpec((None, None, f, cc), lambda ex, s, idx: (layer, ex, 0, jnp.maximum(s - nf, 0)))
    return pl.pallas_call(
        functools.partial(_ffn_body, rows=rows, nf=nf),
        grid_spec=pltpu.PrefetchScalarGridSpec(
            num_scalar_prefetch=1,
            grid=(n_e, nf + nd),
            in_specs=[pl.BlockSpec(memory_space=pl.ANY),
                      pl.BlockSpec((None, rows, 1), lambda ex, s, idx: (ex, 0, 0)),
                      wspec_up, wspec_up, wspec_dn],
            out_specs=pl.BlockSpec((None, rows, cc), lambda ex, s, idx: (ex, 0, jnp.maximum(s - nf, 0))),
            scratch_shapes=[pltpu.VMEM((rows, d // 2), U32), pltpu.VMEM((rows, f), BF16),
                            pltpu.SemaphoreType.DMA],
        ),
        out_shape=jax.ShapeDtypeStruct((n_e, rows, d), F32),
        compiler_params=_cparams("arbitrary", "arbitrary"),
        name="expert_ffn",
    )(idx_flat, hp, gate_col, w_gate, w_up, w_down)


COMBINE_ROWS = 256
COMBINE_SLOTS = 3
COMBINE_MAX_TILES = 6


def _combine_body(idx_ref, ye_ref, x_in_hbm, x_hbm, buf_ref, sem_in, sem_out, *, rows, tiles):
    del x_in_hbm
    tm = COMBINE_ROWS
    base = pl.program_id(0) * rows + pl.program_id(1) * (tiles * tm)

    def gather(k):
        s = k % COMBINE_SLOTS
        for i in range(tm):
            t = idx_ref[base + k * tm + i]
            pltpu.make_async_copy(x_hbm.at[pl.ds(t, 1)], buf_ref.at[s, pl.ds(i, 1)], sem_in.at[s]).start()

    def scatter(k):
        s = k % COMBINE_SLOTS
        for i in range(tm):
            t = idx_ref[base + k * tm + i]
            pltpu.make_async_copy(buf_ref.at[s, pl.ds(i, 1)], x_hbm.at[pl.ds(t, 1)], sem_out.at[s]).start()

    def wait_gather(k):
        s = k % COMBINE_SLOTS
        pltpu.make_async_copy(x_hbm.at[pl.ds(0, tm)], buf_ref.at[s], sem_in.at[s]).wait()

    def wait_scatter(k):
        s = k % COMBINE_SLOTS
        pltpu.make_async_copy(buf_ref.at[s], x_hbm.at[pl.ds(0, tm)], sem_out.at[s]).wait()

    ahead = COMBINE_SLOTS - 1
    for k in range(min(ahead, tiles)):
        gather(k)
    drained = 0
    for k in range(tiles):
        s = k % COMBINE_SLOTS
        wait_gather(k)
        buf_ref[s] = buf_ref[s] + ye_ref[k * tm:(k + 1) * tm, :]
        scatter(k)
        if k + ahead < tiles:
            if k >= 1:
                wait_scatter(k - 1)
                drained = k
            gather(k + ahead)
    for k in range(drained, tiles):
        wait_scatter(k)


def combine(idx_flat, ye, x):
    n_e, rows, d = ye.shape
    tm = COMBINE_ROWS
    n_tiles = rows // tm
    assert rows % tm == 0
    tiles = max(c for c in range(1, COMBINE_MAX_TILES + 1) if n_tiles % c == 0)
    return pl.pallas_call(
        functools.partial(_combine_body, rows=rows, tiles=tiles),
        grid_spec=pltpu.PrefetchScalarGridSpec(
            num_scalar_prefetch=1,
            grid=(n_e, n_tiles // tiles),
            in_specs=[pl.BlockSpec((None, tiles * tm, d), lambda ex, j, idx: (ex, j, 0)),
                      pl.BlockSpec(memory_space=pl.ANY)],
            out_specs=pl.BlockSpec(memory_space=pl.ANY),
            scratch_shapes=[pltpu.VMEM((COMBINE_SLOTS, tm, d), F32), pltpu.SemaphoreType.DMA((COMBINE_SLOTS,)),
                            pltpu.SemaphoreType.DMA((COMBINE_SLOTS,))],
        ),
        out_shape=jax.ShapeDtypeStruct(x.shape, x.dtype),
        input_output_aliases={2: 0},
        compiler_params=_cparams("arbitrary", "arbitrary"),
        name="combine",
    )(idx_flat, ye, x)


def _rope_tables(segments, n):
    half = MLA_ROPE_DIM // 2
    inv = jnp.power(ROPE_THETA, -jnp.arange(0, MLA_ROPE_DIM, 2, dtype=F32) / MLA_ROPE_DIM)
    pos = jnp.concatenate([jnp.tile(jnp.arange(t, dtype=F32), n_seq) for (_, n_seq, t) in segments])
    assert pos.shape[0] == n

    def cs(p):
        ang = p[:, None] * inv[None, :]
        return jnp.cos(ang), jnp.sin(ang)

    c, s = cs(pos)
    z = jnp.zeros((n, 2 * half), F32)
    cos_a = jnp.concatenate([c, c, z], axis=1)
    sin_a = jnp.concatenate([-s, s, z], axis=1)
    cr, sr = cs(jnp.floor(pos / GRID_W))
    cc, sc = cs(pos - GRID_W * jnp.floor(pos / GRID_W))
    cos_b = jnp.concatenate([cr, cr, cc, cc], axis=1)
    sin_b = jnp.concatenate([-sr, sr, -sc, sc], axis=1)
    return cos_a, sin_a, cos_b, sin_b


def _pad_cols(w, width):
    return jnp.pad(w, ((0, 0), (0, width - w.shape[1])))


def _even_layer_weights(w_in, w_uq, w_ukv, q_gain, k_gain):
    o = [0, MLA_Q_RANK, MLA_Q_RANK + MLA_KV_RANK]
    o.append(o[-1] + MLA_ROPE_DIM)
    o.append(o[-1] + GQA_Q_HEADS * GQA_HEAD_DIM)
    o.append(o[-1] + GQA_KV_HEADS * GQA_HEAD_DIM)
    o.append(o[-1] + GQA_KV_HEADS * GQA_HEAD_DIM)
    pieces = [w_in[:, o[0]:o[2]], _pad_cols(w_in[:, o[2]:o[3]], LANES), w_in[:, o[3]:o[6]]]
    w_in_p = _pad_cols(jnp.concatenate(pieces, axis=1), AB_WIDTH).astype(BF16)
    qd = MLA_NOPE_DIM + MLA_ROPE_DIM
    w_uq_p = jnp.pad(w_uq.reshape(MLA_Q_RANK, MLA_HEADS, qd), ((0, 0), (0, 0), (0, 2 * LANES - qd)))
    w_uq_p = w_uq_p.reshape(MLA_Q_RANK, MLA_HEADS * 2 * LANES).astype(BF16)
    gq = _pad_cols(q_gain.reshape(1, qd), 2 * LANES)
    gk = _pad_cols(k_gain.reshape(1, qd), 2 * LANES)
    return w_in_p, w_uq_p, w_ukv.astype(BF16), gq, gk


def _alibi_slopes():
    return jnp.power(2.0, -8.0 * jnp.arange(1, DIL_HEADS + 1, dtype=F32) / DIL_HEADS)


def kernel(x_prompt, x_sample, ln_mix, ln_ffn, w_in_ab, mla_q_norm, mla_w_uq, mla_kv_norm, mla_w_ukv, mla_q_gain, mla_k_gain, gqa_q_gain, gqa_k_gain, w_out_ab, w_in_c, c_q_gain, c_k_gain, w_out_c, w_router, w_gate, w_up, w_down):
    d = x_prompt.shape[-1]
    groups = [x_sample, x_prompt]
    segments, off = [], 0
    for g in groups:
        segments.append((off, g.shape[0], g.shape[1]))
        off += g.shape[0] * g.shape[1]
    n = off
    x = jnp.concatenate([g.reshape(-1, d) for g in groups], axis=0)
    depth = ln_mix.shape[0]
    n_e = w_router.shape[-1]
    caps = [CAPACITY_FACTOR * n_seq * t // n_e for (_, n_seq, t) in segments]
    rows = sum(caps)
    tabs = _rope_tables(segments, n)
    slopes = _alibi_slopes()

    for layer in range(depth):
        i = layer // 2
        h = rmsnorm_rows(x, ln_mix[layer])
        if layer % 2 == 0:
            w_in_p, w_uq_p, w_ukv_b, gq, gk = _even_layer_weights(
                w_in_ab[i], mla_w_uq[i], mla_w_ukv[i], mla_q_gain[i], mla_k_gain[i])
            proj = matmul(h, w_in_p, tn=AB_WIDTH // 2)
            qa, ka, va, qb, kb, vb = mla_gqa_prep(
                proj, w_uq_p, w_ukv_b, mla_q_norm[i].reshape(1, -1), mla_kv_norm[i].reshape(1, -1), gq, gk,
                gqa_q_gain[i].reshape(1, -1), gqa_k_gain[i].reshape(1, -1), tabs)
            o_a = attention(qa, ka, va, segments, MLA_HEADS, MLA_HEADS, 2 * LANES, MLA_V_DIM)
            o_b = attention(qb, kb, vb, segments, GQA_Q_HEADS, GQA_KV_HEADS, GQA_HEAD_DIM, GQA_HEAD_DIM)
            x = matmul_parts([[a, b] for a, b in zip(o_a, o_b)], w_out_ab[i].astype(BF16), x, tn=d // 2)
        else:
            qkv = matmul_heads(h, w_in_c[i].astype(BF16), tn=w_in_c.shape[-1] // 4)
            o = dilated_attention(qkv, c_q_gain[i].reshape(1, -1), c_k_gain[i].reshape(1, -1), slopes, segments)
            x = matmul_parts([[seg] for seg in o], w_out_c[i].astype(BF16), x, tn=d // 2)

        hp, aff = router(x, ln_ffn[layer], w_router[layer])
        aff_t = aff.T
        idx_parts, gate_parts = [], []
        for (off, n_seq, t), cap in zip(segments, caps):
            idx, gate = route_select(aff, aff_t, off, n_seq * t, cap)
            idx_parts.append(idx + off)
            gate_parts.append(gate)
        idx_flat = jnp.concatenate(idx_parts, axis=1).reshape(-1)
        gate_col = jnp.concatenate(gate_parts, axis=1).reshape(n_e, rows, 1)
        ye = expert_ffn(idx_flat, gate_col, hp, w_gate, w_up, w_down, layer, rows)
        x = combine(idx_flat, ye, x)

    outs = []
    for g, (off, n_seq, t) in zip(groups, segments):
        outs.append(x[off:off + n_seq * t].reshape(g.shape))
    return (outs[1], outs[0])
```

```python
import functools
import math

import jax
import jax.numpy as jnp
from jax import lax
from jax.experimental import pallas as pl
from jax.experimental.pallas import tpu as pltpu

F32 = jnp.float32
BF16 = jnp.bfloat16
I32 = jnp.int32
U32 = jnp.uint32

D_MODEL = 2048
NORM_EPS = 1e-6
MASK_VALUE = -1e30
GRID_W = 64

MLA_HEADS = 8
MLA_Q_RANK = 512
MLA_KV_RANK = 256
MLA_NOPE_DIM = 128
MLA_ROPE_DIM = 64
MLA_V_DIM = 128
GQA_Q_HEADS = 8
GQA_KV_HEADS = 2
GQA_HEAD_DIM = 128
ROPE_THETA = 10000.0

DIL_HEADS = 16
DIL_HEAD_DIM = 128
DIL_BRANCHES = ((128, 1), (512, 4), (2048, 16))
DIL_Q_BLOCK = 128
DIL_NORM_CHUNK = 256

N_EXPERTS = 16
CAPACITY_FACTOR = 2

LOG2_E = 1.4426950408889634
LANES = 128
SUBLANES = 8
VMEM_LIMIT_BYTES = 56 * 1024 * 1024

AB_COLS = dict(cq=0, ckv=512, kpe=768, qb=896, kb=1920, vb=2176)
AB_WIDTH = 2560


def _cparams(*sem):
    return pltpu.CompilerParams(dimension_semantics=sem, vmem_limit_bytes=VMEM_LIMIT_BYTES)


def _rmsnorm_body(x_ref, g_ref, o_ref):
    x = x_ref[...]
    ms = jnp.mean(x * x, axis=-1, keepdims=True)
    o_ref[...] = (x * lax.rsqrt(ms + NORM_EPS) * g_ref[...]).astype(o_ref.dtype)


def rmsnorm_rows(x, g, tm=512):
    n, d = x.shape
    return pl.pallas_call(
        _rmsnorm_body,
        grid=(n // tm,),
        in_specs=[pl.BlockSpec((tm, d), lambda i: (i, 0)), pl.BlockSpec((1, d), lambda i: (0, 0))],
        out_specs=pl.BlockSpec((tm, d), lambda i: (i, 0)),
        out_shape=jax.ShapeDtypeStruct((n, d), BF16),
        compiler_params=_cparams("parallel"),
        name="rmsnorm_rows",
    )(x, g.reshape(1, d))


def _matmul_body(x_ref, w_ref, o_ref):
    o_ref[...] = jnp.dot(x_ref[...], w_ref[...], preferred_element_type=F32)


def matmul(x, w, tm=512, tn=None):
    n, k = x.shape
    m = w.shape[1]
    tn = m if tn is None else tn
    return pl.pallas_call(
        _matmul_body,
        grid=(m // tn, n // tm),
        in_specs=[pl.BlockSpec((tm, k), lambda j, i: (i, 0)), pl.BlockSpec((k, tn), lambda j, i: (0, j))],
        out_specs=pl.BlockSpec((tm, tn), lambda j, i: (i, j)),
        out_shape=jax.ShapeDtypeStruct((n, m), F32),
        compiler_params=_cparams("parallel", "parallel"),
        name="matmul",
    )(x, w)


def _matmul_heads_body(x_ref, w_ref, o_ref):
    acc = jnp.dot(x_ref[...], w_ref[...], preferred_element_type=F32)
    for h in range(o_ref.shape[0]):
        o_ref[h] = acc[:, h * LANES:(h + 1) * LANES]


def matmul_heads(x, w, tm=512, tn=None):
    n, k = x.shape
    m = w.shape[1]
    tn = m if tn is None else tn
    return pl.pallas_call(
        _matmul_heads_body,
        grid=(m // tn, n // tm),
        in_specs=[pl.BlockSpec((tm, k), lambda j, i: (i, 0)), pl.BlockSpec((k, tn), lambda j, i: (0, j))],
        out_specs=pl.BlockSpec((tn // LANES, tm, LANES), lambda j, i: (j, i, 0)),
        out_shape=jax.ShapeDtypeStruct((m // LANES, n, LANES), F32),
        compiler_params=_cparams("parallel", "parallel"),
        name="matmul_heads",
    )(x, w)


def _matmul_parts_body(*refs, n_seg, n_part, seg_tiles):
    x_refs = refs[:n_seg * n_part]
    w_ref, r_ref, o_ref = refs[n_seg * n_part:]
    i = pl.program_id(1)
    start = 0
    for s in range(n_seg):
        @pl.when((i >= start) & (i < start + seg_tiles[s]))
        def _segment(s=s):
            acc = r_ref[...]
            k0 = 0
            for p in range(n_part):
                x_ref = x_refs[s * n_part + p]
                if len(x_ref.shape) == 3:
                    x = jnp.concatenate([x_ref[h] for h in range(x_ref.shape[0])], axis=1)
                else:
                    x = x_ref[...]
                acc = acc + jnp.dot(x, w_ref[k0:k0 + x.shape[1], :], preferred_element_type=F32)
                k0 += x.shape[1]
            o_ref[...] = acc
        start += seg_tiles[s]


def matmul_parts(parts, w, res, tm=512, tn=None):
    n, m = res.shape
    k = w.shape[0]
    tn = m if tn is None else tn
    n_seg, n_part = len(parts), len(parts[0])
    seg_tiles = [seg[0].shape[-2] // tm for seg in parts]
    assert sum(seg_tiles) * tm == n and all(seg[0].shape[-2] % tm == 0 for seg in parts)
    in_specs, args, start = [], [], 0
    for s, seg in enumerate(parts):
        for x in seg:
            row = lambda i, start=start, last=seg_tiles[s] - 1: jnp.clip(i - start, 0, last)
            if x.ndim == 3:
                in_specs.append(pl.BlockSpec((x.shape[0], tm, x.shape[2]), lambda j, i, row=row: (0, row(i), 0)))
            else:
                in_specs.append(pl.BlockSpec((tm, x.shape[1]), lambda j, i, row=row: (row(i), 0)))
            args.append(x)
        start += seg_tiles[s]
    in_specs += [pl.BlockSpec((k, tn), lambda j, i: (0, j)), pl.BlockSpec((tm, tn), lambda j, i: (i, j))]
    return pl.pallas_call(
        functools.partial(_matmul_parts_body, n_seg=n_seg, n_part=n_part, seg_tiles=tuple(seg_tiles)),
        grid=(m // tn, n // tm),
        in_specs=in_specs,
        out_specs=pl.BlockSpec((tm, tn), lambda j, i: (i, j)),
        out_shape=jax.ShapeDtypeStruct((n, m), F32),
        compiler_params=_cparams("parallel", "arbitrary"),
        name="matmul_parts",
    )(*args, w, res)


def _rms(x, g, n):
    return x * lax.rsqrt(jnp.sum(x * x, axis=-1, keepdims=True) * (1.0 / n) + NORM_EPS) * g


def _rotate_pairs(x, cos, sin):
    lane = lax.broadcasted_iota(I32, x.shape, 1)
    swapped = jnp.where((lane % 64) < 32, pltpu.roll(x, 96, 1), pltpu.roll(x, 32, 1))
    return x * cos + swapped * sin


def _prep_body(p_ref, wuq_ref, wukv_ref, qn_ref, kvn_ref, gq_ref, gk_ref, gbq_ref, gbk_ref,
               ca_ref, sa_ref, cb_ref, sb_ref,
               qa_ref, ka_ref, va_ref, qb_ref, kb_ref, vb_ref):
    c = AB_COLS
    scale_a = LOG2_E / math.sqrt(MLA_NOPE_DIM + MLA_ROPE_DIM)
    scale_b = LOG2_E / math.sqrt(GQA_HEAD_DIM)
    ca, sa, cb, sb = ca_ref[...], sa_ref[...], cb_ref[...], sb_ref[...]
    gq, gk = gq_ref[...], gk_ref[...]

    cqn = _rms(p_ref[:, c["cq"]:c["cq"] + MLA_Q_RANK], qn_ref[...], MLA_Q_RANK).astype(BF16)
    q = jnp.dot(cqn, wuq_ref[...], preferred_element_type=F32)
    ckvn = _rms(p_ref[:, c["ckv"]:c["ckv"] + MLA_KV_RANK], kvn_ref[...], MLA_KV_RANK).astype(BF16)
    kv = jnp.dot(ckvn, wukv_ref[...], preferred_element_type=F32)

    kpe = p_ref[:, c["kpe"]:c["kpe"] + LANES]
    kpe = _rotate_pairs(_rms(kpe, gk[:, LANES:], MLA_ROPE_DIM), ca, sa).astype(BF16)

    for h in range(MLA_HEADS):
        o = 2 * LANES * h
        qn = _rms(q[:, o:o + LANES], gq[:, :LANES], MLA_NOPE_DIM) * scale_a
        qr = _rotate_pairs(_rms(q[:, o + LANES:o + 2 * LANES], gq[:, LANES:], MLA_ROPE_DIM), ca, sa) * scale_a
        qa_ref[:, o:o + LANES] = qn.astype(BF16)
        qa_ref[:, o + LANES:o + 2 * LANES] = qr.astype(BF16)
        ka_ref[:, o:o + LANES] = _rms(kv[:, o:o + LANES], gk[:, :LANES], MLA_NOPE_DIM).astype(BF16)
        ka_ref[:, o + LANES:o + 2 * LANES] = kpe
        va_ref[:, LANES * h:LANES * (h + 1)] = kv[:, o + LANES:o + 2 * LANES].astype(BF16)

    for h in range(GQA_Q_HEADS):
        x = p_ref[:, c["qb"] + LANES * h:c["qb"] + LANES * (h + 1)]
        x = _rotate_pairs(_rms(x, gbq_ref[...], GQA_HEAD_DIM), cb, sb) * scale_b
        qb_ref[:, LANES * h:LANES * (h + 1)] = x.astype(BF16)
    for h in range(GQA_KV_HEADS):
        x = p_ref[:, c["kb"] + LANES * h:c["kb"] + LANES * (h + 1)]
        x = _rotate_pairs(_rms(x, gbk_ref[...], GQA_HEAD_DIM), cb, sb)
        kb_ref[:, LANES * h:LANES * (h + 1)] = x.astype(BF16)
        vb_ref[:, LANES * h:LANES * (h + 1)] = p_ref[:, c["vb"] + LANES * h:c["vb"] + LANES * (h + 1)].astype(BF16)


def mla_gqa_prep(proj, wuq, wukv, qn, kvn, gq, gk, gbq, gbk, tabs, tm=256):
    n = proj.shape[0]
    row = lambda w: pl.BlockSpec((tm, w), lambda i: (i, 0))
    full = lambda a: pl.BlockSpec(a.shape, lambda i: (0, 0))
    small = [wuq, wukv, qn, kvn, gq, gk, gbq, gbk]
    widths = [2 * LANES * MLA_HEADS, 2 * LANES * MLA_HEADS, LANES * MLA_HEADS,
              LANES * GQA_Q_HEADS, LANES * GQA_KV_HEADS, LANES * GQA_KV_HEADS]
    return pl.pallas_call(
        _prep_body,
        grid=(n // tm,),
        in_specs=[row(AB_WIDTH)] + [full(a) for a in small] + [row(LANES)] * 4,
        out_specs=[row(w) for w in widths],
        out_shape=[jax.ShapeDtypeStruct((n, w), BF16) for w in widths],
        compiler_params=_cparams("parallel"),
        name="mla_gqa_prep",
    )(proj, *small, *tabs)


def _attn_body(q_ref, k_ref, v_ref, o_ref, *, tk):
    q = q_ref[...]
    tq = q.shape[0]
    dv = v_ref.shape[1]
    n_kv = k_ref.shape[0] // tk

    def step(j, carry):
        m, l, acc = carry
        r0 = pl.multiple_of(j * tk, tk)
        k = k_ref[pl.ds(r0, tk), :]
        v = v_ref[pl.ds(r0, tk), :]
        s = lax.dot_general(q, k, (((1,), (1,)), ((), ())), preferred_element_type=F32)
        m_new = jnp.maximum(m, jnp.max(s, axis=-1, keepdims=True))
        alpha = jnp.exp2(m - m_new)
        p = jnp.exp2(s - m_new)
        l = alpha * l + jnp.sum(p, axis=-1, keepdims=True)
        acc = alpha * acc + jnp.dot(p.astype(BF16), v, preferred_element_type=F32)
        return m_new, l, acc

    init = (jnp.full((tq, 1), -jnp.inf, F32), jnp.zeros((tq, 1), F32), jnp.zeros((tq, dv), F32))
    m, l, acc = lax.fori_loop(0, n_kv, step, init)
    o_ref[...] = (acc / l).astype(o_ref.dtype)


def attention(q, k, v, segments, hq, hk, dk, dv, tq=2048, tk=2048):
    g = hq // hk
    outs = []
    for (off, n_seq, t) in segments:
        tq_s, tk_s = min(tq, t), min(tk, t)
        assert off % t == 0 and t % tq_s == 0 and t % tk_s == 0
        qblk = lambda b, h, i, off=off, t=t, tq_s=tq_s: ((off + b * t) // tq_s + i, h)
        oblk = lambda b, h, i, t=t, tq_s=tq_s: ((b * t) // tq_s + i, h)
        kvblk = lambda b, h, i, off=off, t=t: (off // t + b, h // g)
        outs.append(pl.pallas_call(
            functools.partial(_attn_body, tk=tk_s),
            grid=(n_seq, hq, t // tq_s),
            in_specs=[pl.BlockSpec((tq_s, dk), qblk), pl.BlockSpec((t, dk), kvblk), pl.BlockSpec((t, dv), kvblk)],
            out_specs=pl.BlockSpec((tq_s, dv), oblk),
            out_shape=jax.ShapeDtypeStruct((n_seq * t, hq * dv), BF16),
            compiler_params=_cparams("parallel", "parallel", "arbitrary"),
            name="attention",
        )(q, k, v))
    return outs


def _dilated_body(slope_ref, q_ref, k_ref, v_ref, gq_ref, gk_ref, o_ref, qn_ref, kn_ref, bias_ref, ob_ref, lb_ref,
                  *, t, tile):
    h = pl.program_id(1)
    slope = slope_ref[h]
    gq, gk = gq_ref[...], gk_ref[...]
    scale = 1.0 / math.sqrt(DIL_HEAD_DIM)
    qb = DIL_Q_BLOCK
    wmax = bias_ref.shape[-1]

    def norm_chunk(c, carry):
        rows = pl.ds(pl.multiple_of(c * DIL_NORM_CHUNK, DIL_NORM_CHUNK), DIL_NORM_CHUNK)
        qn_ref[rows, :] = _rms(q_ref[rows, :], gq, DIL_HEAD_DIM) * scale
        kn_ref[rows, :] = _rms(k_ref[rows, :], gk, DIL_HEAD_DIM)
        return carry

    lax.fori_loop(0, t // DIL_NORM_CHUNK, norm_chunk, 0)

    rel0 = lax.broadcasted_iota(I32, (qb, wmax), 1) - lax.broadcasted_iota(I32, (qb, wmax), 0)
    for bi, (window, d) in enumerate(DIL_BRANCHES):
        nh = (window // 2) // d
        for var in range(3):
            dist = jnp.abs(rel0 - nh * var)
            bias_ref[bi, var] = jnp.where(dist <= nh, -slope * (d * dist).astype(F32), MASK_VALUE)

    def tile_body(ti, carry):
        base = pl.multiple_of(ti * tile, tile)
        for bi, (window, d) in enumerate(DIL_BRANCHES):
            nh = (window // 2) // d
            ln = t // d
            w = min(qb + 2 * nh, ln)
            for r in range(d):
                for bb in range(tile // (qb * d)):
                    p0 = ti * (tile // d) + qb * bb
                    ws = jnp.clip(p0 - nh, 0, ln - w)
                    qn = qn_ref[pl.ds(base + r + d * qb * bb, qb, stride=d), :].astype(BF16)
                    kn = kn_ref[pl.ds(r + d * ws, w, stride=d), :].astype(BF16)
                    vw = v_ref[pl.ds(r + d * ws, w, stride=d), :]
                    s = lax.dot_general(qn, kn, (((1,), (1,)), ((), ())), preferred_element_type=F32)
                    s = s + bias_ref[bi, (p0 - ws) // nh, :, :w]
                    m = jnp.max(s, axis=-1, keepdims=True)
                    p = jnp.exp(s - m).astype(BF16)
                    v1 = jnp.concatenate([vw.astype(BF16), jnp.ones((w, DIL_HEAD_DIM), BF16)], axis=1)
                    od = jnp.dot(p, v1, preferred_element_type=F32)
                    den = od[:, DIL_HEAD_DIM:]
                    rows = pl.ds(r + d * qb * bb, qb, stride=d)
                    ob_ref[bi, rows, :] = od[:, :DIL_HEAD_DIM] / den
                    lb_ref[bi, rows, :] = m + jnp.log(den)
        l0, l1, l2 = lb_ref[0], lb_ref[1], lb_ref[2]
        mx = jnp.maximum(jnp.maximum(l0, l1), l2)
        w0, w1, w2 = jnp.exp(l0 - mx), jnp.exp(l1 - mx), jnp.exp(l2 - mx)
        inv = 1.0 / (w0 + w1 + w2)
        merged = (w0 * inv) * ob_ref[0] + (w1 * inv) * ob_ref[1] + (w2 * inv) * ob_ref[2]
        o_ref[pl.ds(base, tile), :] = merged.astype(o_ref.dtype)
        return carry

    lax.fori_loop(0, t // tile, tile_body, 0)


def dilated_attention(qkv, gq, gk, slopes, segments):
    hd = DIL_HEAD_DIM
    max_d = max(d for _, d in DIL_BRANCHES)
    tile = DIL_Q_BLOCK * max_d
    max_nh = max((window // 2) // d for window, d in DIL_BRANCHES)
    assert max_nh <= DIL_Q_BLOCK
    wmax = DIL_Q_BLOCK + 2 * max_nh
    n_br = len(DIL_BRANCHES)
    outs = []
    for (off, n_seq, t) in segments:
        assert off % t == 0 and t % tile == 0 and t % DIL_NORM_CHUNK == 0
        blk = lambda c0: pl.BlockSpec((None, t, hd), lambda b, h, s, off=off, t=t, c0=c0: (c0 + h, off // t + b, 0))
        vec = pl.BlockSpec((1, hd), lambda b, h, s: (0, 0))
        outs.append(pl.pallas_call(
            functools.partial(_dilated_body, t=t, tile=tile),
            grid_spec=pltpu.PrefetchScalarGridSpec(
                num_scalar_prefetch=1,
                grid=(n_seq, DIL_HEADS),
                in_specs=[blk(0), blk(DIL_HEADS), blk(2 * DIL_HEADS), vec, vec],
                out_specs=pl.BlockSpec((None, t, hd), lambda b, h, s: (h, b, 0)),
                scratch_shapes=[pltpu.VMEM((t, hd), F32), pltpu.VMEM((t, hd), F32),
                                pltpu.VMEM((n_br, 3, DIL_Q_BLOCK, wmax), F32),
                                pltpu.VMEM((n_br, tile, hd), F32), pltpu.VMEM((n_br, tile, hd), F32)],
            ),
            out_shape=jax.ShapeDtypeStruct((DIL_HEADS, n_seq * t, hd), BF16),
            compiler_params=_cparams("parallel", "arbitrary"),
            name="dilated_attention",
        )(slopes, qkv, qkv, qkv, gq, gk))
    return outs


def _split_bf16(x):
    hi = x.astype(BF16)
    lo = (x - hi.astype(F32)).astype(BF16)
    return hi, lo


def _router_body(x_ref, g_ref, w_ref, hp_ref, aff_ref):
    x = x_ref[...]
    d = x.shape[1]
    ms = jnp.mean(x * x, axis=-1, keepdims=True)
    hn = x * lax.rsqrt(ms + NORM_EPS) * g_ref[...]
    xh, xl = _split_bf16(hn)
    wh, wl = _split_bf16(w_ref[...])
    logits = (jnp.dot(xh, wh, preferred_element_type=F32) + jnp.dot(xl, wh, preferred_element_type=F32)
              + jnp.dot(xh, wl, preferred_element_type=F32))
    mx = jnp.max(logits, axis=-1, keepdims=True)
    ex = jnp.exp(logits - mx)
    aff_ref[...] = ex / jnp.sum(ex, axis=-1, keepdims=True)
    r = pltpu.bitcast(xh.astype(F32), U32)
    hp_ref[...] = (r[:, :d // 2] >> 16) | (r[:, d // 2:] & jnp.uint32(0xFFFF0000))


def router(x, g, w_router, tm=512):
    n, d = x.shape
    e = w_router.shape[1]
    return pl.pallas_call(
        _router_body,
        grid=(n // tm,),
        in_specs=[pl.BlockSpec((tm, d), lambda i: (i, 0)), pl.BlockSpec((1, d), lambda i: (0, 0)),
                  pl.BlockSpec((d, e), lambda i: (0, 0))],
        out_specs=[pl.BlockSpec((tm, d // 2), lambda i: (i, 0)), pl.BlockSpec((tm, e), lambda i: (i, 0))],
        out_shape=[jax.ShapeDtypeStruct((n, d // 2), U32), jax.ShapeDtypeStruct((n, e), F32)],
        compiler_params=_cparams("parallel"),
        name="router",
    )(x, g.reshape(1, d), w_router)


ROUTE_CHUNK = 128
CUMSUM_CHUNK = 256


def _route_pos_body(aff_ref, posm_ref, pos_ref, sel_ref, *, cap):
    e, n = aff_ref.shape
    bits = pltpu.bitcast(aff_ref[...], I32)
    tok = lax.broadcasted_iota(I32, (e, n), 1)

    def count(mask):
        return jnp.sum(mask.astype(F32), axis=1, keepdims=True)

    def value_bit(i, thr):
        cand = thr | (jnp.int32(1) << (30 - i))
        return jnp.where(count(bits >= cand) >= cap, cand, thr)

    thr = lax.fori_loop(0, 31, value_bit, jnp.zeros((e, 1), I32))
    gt = bits > thr
    eq = bits == thr
    need = cap - count(gt)

    def index_bit(i, j):
        cand = j | (jnp.int32(1) << (n.bit_length() - 1 - i))
        return jnp.where(count(eq & (tok < cand)) < need, cand, j)

    last = lax.fori_loop(0, n.bit_length(), index_bit, jnp.zeros((e, 1), I32))
    sel = gt | (eq & (tok <= last))
    sel_ref[...] = sel.astype(F32)

    ck = CUMSUM_CHUNK
    triu = (lax.broadcasted_iota(I32, (ck, ck), 0) < lax.broadcasted_iota(I32, (ck, ck), 1)).astype(BF16)

    def cumsum_chunk(c, carry):
        c0 = pl.multiple_of(c * ck, ck)
        s = sel_ref[:, pl.ds(c0, ck)]
        excl = jnp.dot(s.astype(BF16), triu, preferred_element_type=F32) + carry
        pos_ref[:, pl.ds(c0, ck)] = excl.astype(I32)
        return carry + jnp.sum(s, axis=1, keepdims=True)

    lax.fori_loop(0, n // ck, cumsum_chunk, jnp.zeros((e, 1), F32))
    posm_ref[...] = jnp.where(sel_ref[...] > 0.5, pos_ref[...], -1)


def _route_compact_body(cs_ref, posm_ref, aff_ref, idx_ref, gate_ref, acci_ref, accg_ref):
    n, e = posm_ref.shape
    rc = ROUTE_CHUNK
    acci_ref[...] = jnp.zeros_like(acci_ref)
    accg_ref[...] = jnp.zeros_like(accg_ref)
    lane = lax.broadcasted_iota(I32, (rc, LANES), 1)
    row = lax.broadcasted_iota(I32, (rc, LANES), 0)

    def chunk(c, carry):
        r0 = pl.multiple_of(c * rc, rc)
        pm = posm_ref[pl.ds(r0, rc), :]
        af = aff_ref[pl.ds(r0, rc), :]
        tok = (r0 + row).astype(F32)
        for ex in range(e):
            pe = jnp.broadcast_to(pm[:, ex:ex + 1], (rc, LANES))
            ge = jnp.broadcast_to(af[:, ex:ex + 1], (rc, LANES))
            kb = cs_ref[c * e + ex] // LANES
            for half in range(2):
                s0 = pl.multiple_of((kb + half) * LANES, LANES)
                hit = pe == (s0 + lane)
                ci = jnp.where(hit, tok, 0.0).reshape(rc // SUBLANES, SUBLANES, LANES).sum(axis=0)
                cg = jnp.where(hit, ge, 0.0).reshape(rc // SUBLANES, SUBLANES, LANES).sum(axis=0)
                acci_ref[ex, :, pl.ds(s0, LANES)] += ci
                accg_ref[ex, :, pl.ds(s0, LANES)] += cg
        return carry

    lax.fori_loop(0, n // rc, chunk, 0)
    idx_ref[...] = jnp.sum(acci_ref[...], axis=1).astype(I32)
    gate_ref[...] = jnp.sum(accg_ref[...], axis=1)


def route_select(aff, aff_t, off, n_tok, cap):
    e = aff.shape[1]
    assert off % n_tok == 0 and n_tok % CUMSUM_CHUNK == 0 and cap % LANES == 0
    lane_dense = pl.BlockSpec((e, n_tok), lambda i: (0, 0))
    posm_t, pos_t = pl.pallas_call(
        functools.partial(_route_pos_body, cap=cap),
        grid=(1,),
        in_specs=[pl.BlockSpec((e, n_tok), lambda i: (0, off // n_tok))],
        out_specs=[lane_dense, lane_dense],
        out_shape=[jax.ShapeDtypeStruct((e, n_tok), I32), jax.ShapeDtypeStruct((e, n_tok), I32)],
        scratch_shapes=[pltpu.VMEM((e, n_tok), F32)],
        compiler_params=_cparams("arbitrary"),
        name="route_pos",
    )(aff_t)
    posm = posm_t.T
    cs = pos_t[:, ::ROUTE_CHUNK].T
    cpad = cap + 2 * LANES
    idx, gate = pl.pallas_call(
        _route_compact_body,
        grid_spec=pltpu.PrefetchScalarGridSpec(
            num_scalar_prefetch=1,
            grid=(1,),
            in_specs=[pl.BlockSpec((n_tok, e), lambda i, cs: (0, 0)),
                      pl.BlockSpec((n_tok, e), lambda i, cs: (off // n_tok, 0))],
            out_specs=[pl.BlockSpec((e, cpad), lambda i, cs: (0, 0)), pl.BlockSpec((e, cpad), lambda i, cs: (0, 0))],
            scratch_shapes=[pltpu.VMEM((e, SUBLANES, cpad), F32), pltpu.VMEM((e, SUBLANES, cpad), F32)],
        ),
        out_shape=[jax.ShapeDtypeStruct((e, cpad), I32), jax.ShapeDtypeStruct((e, cpad), F32)],
        compiler_params=_cparams("arbitrary"),
        name="route_compact",
    )(cs.reshape(-1), posm, aff)
    return idx[:, :cap], gate[:, :cap]


FFN_COL_CHUNK = 256
FFN_ROW_CHUNK = 512
FFN_GATHER_UNROLL = 32


def _unpack_bf16_pair(u):
    lo = pltpu.bitcast(u << 16, F32).astype(BF16)
    hi = pltpu.bitcast(u & jnp.uint32(0xFFFF0000), F32).astype(BF16)
    return lo, hi


def _ffn_body(idx_ref, hp_hbm, gate_ref, wg_ref, wu_ref, wd_ref, ye_ref, xg_ref, hid_ref, sem, *, rows, nf):
    ex = pl.program_id(0)
    s = pl.program_id(1)
    half = wg_ref.shape[0] // 2
    fc = wg_ref.shape[1]
    rc = FFN_ROW_CHUNK

    @pl.when(s == 0)
    def _gather():
        def issue(g, carry):
            g0 = pl.multiple_of(g * FFN_GATHER_UNROLL, FFN_GATHER_UNROLL)
            dst = xg_ref.at[pl.ds(g0, FFN_GATHER_UNROLL)]
            for u in range(FFN_GATHER_UNROLL):
                t = idx_ref[ex * rows + g0 + u]
                pltpu.make_async_copy(hp_hbm.at[pl.ds(t, 1)], dst.at[pl.ds(u, 1)], sem).start()
            return carry
        lax.fori_loop(0, rows // FFN_GATHER_UNROLL, issue, 0)
        pltpu.make_async_copy(hp_hbm.at[pl.ds(0, rows)], xg_ref, sem).wait()

    @pl.when(s < nf)
    def _gate_up():
        wg_lo, wg_hi = wg_ref[:half, :].astype(BF16), wg_ref[half:, :].astype(BF16)
        wu_lo, wu_hi = wu_ref[:half, :].astype(BF16), wu_ref[half:, :].astype(BF16)
        col = pl.multiple_of(s * fc, fc)
        for r in range(rows // rc):
            x_lo, x_hi = _unpack_bf16_pair(xg_ref[r * rc:(r + 1) * rc, :])
            g = (jnp.dot(x_lo, wg_lo, preferred_element_type=F32)
                 + jnp.dot(x_hi, wg_hi, preferred_element_type=F32))
            u = (jnp.dot(x_lo, wu_lo, preferred_element_type=F32)
                 + jnp.dot(x_hi, wu_hi, preferred_element_type=F32))
            hid_ref[r * rc:(r + 1) * rc, pl.ds(col, fc)] = (g * jax.nn.sigmoid(g) * u).astype(BF16)

    @pl.when(s >= nf)
    def _down():
        wd = wd_ref[...].astype(BF16)
        for r in range(rows // rc):
            y = jnp.dot(hid_ref[r * rc:(r + 1) * rc, :], wd, preferred_element_type=F32)
            ye_ref[r * rc:(r + 1) * rc, :] = y * gate_ref[r * rc:(r + 1) * rc, :]


def expert_ffn(idx_flat, gate_col, hp, w_gate, w_up, w_down, layer, rows):
    n_e, d, f = w_gate.shape[1:]
    cc = FFN_COL_CHUNK
    nf, nd = f // cc, d // cc
    assert rows % FFN_ROW_CHUNK == 0
    wspec_up = pl.BlockSpec((None, None, d, cc), lambda ex, s, idx: (layer, ex, 0, jnp.minimum(s, nf - 1)))
    wspec_dn = pl.BlockSpec((None, None, f, cc), lambda ex, s, idx: (layer, ex, 0, jnp.maximum(s - nf, 0)))
    return pl.pallas_call(
        functools.partial(_ffn_body, rows=rows, nf=nf),
        grid_spec=pltpu.PrefetchScalarGridSpec(
            num_scalar_prefetch=1,
            grid=(n_e, nf + nd),
            in_specs=[pl.BlockSpec(memory_space=pl.ANY),
                      pl.BlockSpec((None, rows, 1), lambda ex, s, idx: (ex, 0, 0)),
                      wspec_up, wspec_up, wspec_dn],
            out_specs=pl.BlockSpec((None, rows, cc), lambda ex, s, idx: (ex, 0, jnp.maximum(s - nf, 0))),
            scratch_shapes=[pltpu.VMEM((rows, d // 2), U32), pltpu.VMEM((rows, f), BF16),
                            pltpu.SemaphoreType.DMA],
        ),
        out_shape=jax.ShapeDtypeStruct((n_e, rows, d), F32),
        compiler_params=_cparams("arbitrary", "arbitrary"),
        name="expert_ffn",
    )(idx_flat, hp, gate_col, w_gate, w_up, w_down)


COMBINE_ROWS = 256
COMBINE_SLOTS = 3
COMBINE_MAX_TILES = 6


def _combine_body(idx_ref, ye_ref, x_in_hbm, x_hbm, buf_ref, sem_in, sem_out, *, rows, tiles):
    del x_in_hbm
    tm = COMBINE_ROWS
    base = pl.program_id(0) * rows + pl.program_id(1) * (tiles * tm)

    def gather(k):
        s = k % COMBINE_SLOTS
        for i in range(tm):
            t = idx_ref[base + k * tm + i]
            pltpu.make_async_copy(x_hbm.at[pl.ds(t, 1)], buf_ref.at[s, pl.ds(i, 1)], sem_in.at[s]).start()

    def scatter(k):
        s = k % COMBINE_SLOTS
        for i in range(tm):
            t = idx_ref[base + k * tm + i]
            pltpu.make_async_copy(buf_ref.at[s, pl.ds(i, 1)], x_hbm.at[pl.ds(t, 1)], sem_out.at[s]).start()

    def wait_gather(k):
        s = k % COMBINE_SLOTS
        pltpu.make_async_copy(x_hbm.at[pl.ds(0, tm)], buf_ref.at[s], sem_in.at[s]).wait()

    def wait_scatter(k):
        s = k % COMBINE_SLOTS
        pltpu.make_async_copy(buf_ref.at[s], x_hbm.at[pl.ds(0, tm)], sem_out.at[s]).wait()

    ahead = COMBINE_SLOTS - 1
    for k in range(min(ahead, tiles)):
        gather(k)
    drained = 0
    for k in range(tiles):
        s = k % COMBINE_SLOTS
        wait_gather(k)
        buf_ref[s] = buf_ref[s] + ye_ref[k * tm:(k + 1) * tm, :]
        scatter(k)
        if k + ahead < tiles:
            if k >= 1:
                wait_scatter(k - 1)
                drained = k
            gather(k + ahead)
    for k in range(drained, tiles):
        wait_scatter(k)


def combine(idx_flat, ye, x):
    n_e, rows, d = ye.shape
    tm = COMBINE_ROWS
    n_tiles = rows // tm
    assert rows % tm == 0
    tiles = max(c for c in range(1, COMBINE_MAX_TILES + 1) if n_tiles % c == 0)
    return pl.pallas_call(
        functools.partial(_combine_body, rows=rows, tiles=tiles),
        grid_spec=pltpu.PrefetchScalarGridSpec(
            num_scalar_prefetch=1,
            grid=(n_e, n_tiles // tiles),
            in_specs=[pl.BlockSpec((None, tiles * tm, d), lambda ex, j, idx: (ex, j, 0)),
                      pl.BlockSpec(memory_space=pl.ANY)],
            out_specs=pl.BlockSpec(memory_space=pl.ANY),
            scratch_shapes=[pltpu.VMEM((COMBINE_SLOTS, tm, d), F32), pltpu.SemaphoreType.DMA((COMBINE_SLOTS,)),
                            pltpu.SemaphoreType.DMA((COMBINE_SLOTS,))],
        ),
        out_shape=jax.ShapeDtypeStruct(x.shape, x.dtype),
        input_output_aliases={2: 0},
        compiler_params=_cparams("arbitrary", "arbitrary"),
        name="combine",
    )(idx_flat, ye, x)


def _rope_tables(segments, n):
    half = MLA_ROPE_DIM // 2
    inv = jnp.power(ROPE_THETA, -jnp.arange(0, MLA_ROPE_DIM, 2, dtype=F32) / MLA_ROPE_DIM)
    pos = jnp.concatenate([jnp.tile(jnp.arange(t, dtype=F32), n_seq) for (_, n_seq, t) in segments])
    assert pos.shape[0] == n

    def cs(p):
        ang = p[:, None] * inv[None, :]
        return jnp.cos(ang), jnp.sin(ang)

    c, s = cs(pos)
    z = jnp.zeros((n, 2 * half), F32)
    cos_a = jnp.concatenate([c, c, z], axis=1)
    sin_a = jnp.concatenate([-s, s, z], axis=1)
    cr, sr = cs(jnp.floor(pos / GRID_W))
    cc, sc = cs(pos - GRID_W * jnp.floor(pos / GRID_W))
    cos_b = jnp.concatenate([cr, cr, cc, cc], axis=1)
    sin_b = jnp.concatenate([-sr, sr, -sc, sc], axis=1)
    return cos_a, sin_a, cos_b, sin_b


def _pad_cols(w, width):
    return jnp.pad(w, ((0, 0), (0, width - w.shape[1])))


def _even_layer_weights(w_in, w_uq, w_ukv, q_gain, k_gain):
    o = [0, MLA_Q_RANK, MLA_Q_RANK + MLA_KV_RANK]
    o.append(o[-1] + MLA_ROPE_DIM)
    o.append(o[-1] + GQA_Q_HEADS * GQA_HEAD_DIM)
    o.append(o[-1] + GQA_KV_HEADS * GQA_HEAD_DIM)
    o.append(o[-1] + GQA_KV_HEADS * GQA_HEAD_DIM)
    pieces = [w_in[:, o[0]:o[2]], _pad_cols(w_in[:, o[2]:o[3]], LANES), w_in[:, o[3]:o[6]]]
    w_in_p = _pad_cols(jnp.concatenate(pieces, axis=1), AB_WIDTH).astype(BF16)
    qd = MLA_NOPE_DIM + MLA_ROPE_DIM
    w_uq_p = jnp.pad(w_uq.reshape(MLA_Q_RANK, MLA_HEADS, qd), ((0, 0), (0, 0), (0, 2 * LANES - qd)))
    w_uq_p = w_uq_p.reshape(MLA_Q_RANK, MLA_HEADS * 2 * LANES).astype(BF16)
    gq = _pad_cols(q_gain.reshape(1, qd), 2 * LANES)
    gk = _pad_cols(k_gain.reshape(1, qd), 2 * LANES)
    return w_in_p, w_uq_p, w_ukv.astype(BF16), gq, gk


def _alibi_slopes():
    return jnp.power(2.0, -8.0 * jnp.arange(1, DIL_HEADS + 1, dtype=F32) / DIL_HEADS)


def kernel(x_prompt, x_sample, ln_mix, ln_ffn, w_in_ab, mla_q_norm, mla_w_uq, mla_kv_norm, mla_w_ukv, mla_q_gain, mla_k_gain, gqa_q_gain, gqa_k_gain, w_out_ab, w_in_c, c_q_gain, c_k_gain, w_out_c, w_router, w_gate, w_up, w_down):
    d = x_prompt.shape[-1]
    groups = [x_sample, x_prompt]
    segments, off = [], 0
    for g in groups:
        segments.append((off, g.shape[0], g.shape[1]))
        off += g.shape[0] * g.shape[1]
    n = off
    x = jnp.concatenate([g.reshape(-1, d) for g in groups], axis=0)
    depth = ln_mix.shape[0]
    n_e = w_router.shape[-1]
    caps = [CAPACITY_FACTOR * n_seq * t // n_e for (_, n_seq, t) in segments]
    rows = sum(caps)
    tabs = _rope_tables(segments, n)
    slopes = _alibi_slopes()

    for layer in range(depth):
        i = layer // 2
        h = rmsnorm_rows(x, ln_mix[layer])
        if layer % 2 == 0:
            w_in_p, w_uq_p, w_ukv_b, gq, gk = _even_layer_weights(
                w_in_ab[i], mla_w_uq[i], mla_w_ukv[i], mla_q_gain[i], mla_k_gain[i])
            proj = matmul(h, w_in_p)
            qa, ka, va, qb, kb, vb = mla_gqa_prep(
                proj, w_uq_p, w_ukv_b, mla_q_norm[i].reshape(1, -1), mla_kv_norm[i].reshape(1, -1), gq, gk,
                gqa_q_gain[i].reshape(1, -1), gqa_k_gain[i].reshape(1, -1), tabs)
            o_a = attention(qa, ka, va, segments, MLA_HEADS, MLA_HEADS, 2 * LANES, MLA_V_DIM)
            o_b = attention(qb, kb, vb, segments, GQA_Q_HEADS, GQA_KV_HEADS, GQA_HEAD_DIM, GQA_HEAD_DIM)
            x = matmul_parts([[a, b] for a, b in zip(o_a, o_b)], w_out_ab[i].astype(BF16), x)
        else:
            qkv = matmul_heads(h, w_in_c[i].astype(BF16), tn=w_in_c.shape[-1] // 2)
            o = dilated_attention(qkv, c_q_gain[i].reshape(1, -1), c_k_gain[i].reshape(1, -1), slopes, segments)
            x = matmul_parts([[seg] for seg in o], w_out_c[i].astype(BF16), x)

        hp, aff = router(x, ln_ffn[layer], w_router[layer])
        aff_t = aff.T
        idx_parts, gate_parts = [], []
        for (off, n_seq, t), cap in zip(segments, caps):
            idx, gate = route_select(aff, aff_t, off, n_seq * t, cap)
            idx_parts.append(idx + off)
            gate_parts.append(gate)
        idx_flat = jnp.concatenate(idx_parts, axis=1).reshape(-1)
        gate_col = jnp.concatenate(gate_parts, axis=1).reshape(n_e, rows, 1)
        ye = expert_ffn(idx_flat, gate_col, hp, w_gate, w_up, w_down, layer, rows)
        x = combine(idx_flat, ye, x)

    outs = []
    for g, (off, n_seq, t) in zip(groups, segments):
        outs.append(x[off:off + n_seq * t].reshape(g.shape))
    return (outs[1], outs[0])
```
